```python
import math
import jax, jax.numpy as jnp
from jax import lax
import numpy as np

D_MODEL = 1024
BATCH = 4
SEQ = 8192
DEPTH = 1
DEC_BATCH = 16
DEC_SEQ = 16
PAST_LEN = 4096

CHUNK = 64
QBLOCK = 128
EPS = 1e-6
MLA_HEADS = 8
MLA_NOPE = 64
MLA_ROPE = 32
MLA_V = 64
Q_LORA = 256
KV_LORA = 256
ROPE_THETA = 10000.0
MLA_SCALE = (MLA_NOPE + MLA_ROPE) ** -0.5
DIFF_HEADS = 4
DIFF_DH = 64
DIFF_V = 2 * DIFF_DH
DIFF_SCALE = DIFF_DH ** -0.5
NUM_BUCKETS = 32
MAX_DISTANCE = 128
MLA_WIDTH = MLA_HEADS * MLA_V
DIFF_WIDTH = DIFF_HEADS * DIFF_V
MIX_WIDTH = MLA_WIDTH + DIFF_WIDTH
DIFF_QK = DIFF_HEADS * 2 * DIFF_DH
IN_SIZES = (Q_LORA, KV_LORA, MLA_ROPE, DIFF_QK, DIFF_QK, DIFF_WIDTH)
IN_WIDTH = Q_LORA + KV_LORA + MLA_ROPE + 2 * DIFF_QK + DIFF_WIDTH
N_EXPERTS = 256
TOP_K = 8
N_GROUPS = 8
TOPK_GROUPS = 4
EXPERT_FF = 256
SHARED_FF = 256
ROUTE_SCALE = 2.5
ROW_BLOCK = 128

kernel_name = 'hybrid_mla_diffattn_moe_stream_step'


def _rmsnorm(x, g):
    xf = x.astype(jnp.float32)
    y = xf * lax.rsqrt(jnp.mean(xf * xf, axis=-1, keepdims=True) + EPS)
    return (y * g.astype(jnp.float32)).astype(x.dtype)


def _rope(x, pos):
    half = x.shape[-1] // 2
    inv = ROPE_THETA ** (-jnp.arange(half, dtype=jnp.float32) / half)
    ang = pos.astype(jnp.float32)[:, None] * inv
    shp = (pos.shape[0],) + (1,) * (x.ndim - 3) + (half,)
    cos, sin = jnp.cos(ang).reshape(shp), jnp.sin(ang).reshape(shp)
    xf = x.astype(jnp.float32)
    x1, x2 = xf[..., :half], xf[..., half:]
    return jnp.concatenate([x1 * cos - x2 * sin, x2 * cos + x1 * sin], axis=-1).astype(x.dtype)


def _rel_bucket(rel):
    nb = NUM_BUCKETS // 2
    max_exact = nb // 2
    n = jnp.abs(rel)
    nf = jnp.maximum(n, 1).astype(jnp.float32)
    large = max_exact + (jnp.log(nf / max_exact) / math.log(MAX_DISTANCE / max_exact)
                         * (nb - max_exact)).astype(jnp.int32)
    large = jnp.minimum(large, nb - 1)
    return jnp.where(rel > 0, nb, 0) + jnp.where(n < max_exact, n, large)


def _chunk_mask(q_pos, k_pos):
    return (k_pos[None, :] // CHUNK) <= (q_pos[:, None] // CHUNK)


def _split_cols(a, sizes):
    out, start = [], 0
    for sz in sizes:
        out.append(a[..., start:start + sz])
        start += sz
    return out


def _sweep(fn, q_pos, *qs):
    s = q_pos.shape[0]
    blk = QBLOCK if s % QBLOCK == 0 else s
    nb = s // blk

    def split(a):
        return jnp.moveaxis(a.reshape(a.shape[0], nb, blk, *a.shape[2:]), 1, 0)

    out = lax.map(lambda args: fn(*args), (q_pos.reshape(nb, blk),) + tuple(split(a) for a in qs))
    out = jnp.moveaxis(out, 0, 1)
    return out.reshape(out.shape[0], s, *out.shape[3:])


def _mixers(h, q_pos, past, p, l, lam_init):
    b, s, _ = h.shape
    c_q, c_kv, k_pe, dq, dk, dv = _split_cols(h @ p['w_in'][l], IN_SIZES)
    c_kv = _rmsnorm(c_kv, p['mla_kv_norm'][l])
    k_pe = _rope(k_pe, q_pos)
    q = (_rmsnorm(c_q, p['mla_q_norm'][l]) @ p['w_uq'][l]).reshape(b, s, MLA_HEADS, MLA_NOPE + MLA_ROPE)
    q_nope, q_pe = q[..., :MLA_NOPE], _rope(q[..., MLA_NOPE:], q_pos)
    dq = dq.reshape(b, s, DIFF_HEADS, 2, DIFF_DH)
    dk = dk.reshape(b, s, DIFF_HEADS, 2, DIFF_DH)
    dv = dv.reshape(b, s, DIFF_HEADS, DIFF_V)
    new_rows = (c_kv, k_pe, dk, dv)
    if past is None:
        k_pos = q_pos
        lat_all, kpe_all, dk_all, dv_all = new_rows
    else:
        k_pos = jnp.concatenate([jnp.arange(past[0].shape[1], dtype=jnp.int32), q_pos])
        lat_all, kpe_all, dk_all, dv_all = (jnp.concatenate([a, n.astype(a.dtype)], axis=1)
                                            for a, n in zip(past, new_rows))
    kv = (lat_all @ p['w_ukv'][l]).reshape(b, -1, MLA_HEADS, MLA_NOPE + MLA_V)
    k_nope, v_mla = kv[..., :MLA_NOPE], kv[..., MLA_NOPE:]

    def mla_blk(qp, qn, qr):
        sc = (jnp.einsum('bqhd,bkhd->bhqk', qn, k_nope)
              + jnp.einsum('bqhr,bkr->bhqk', qr, kpe_all)).astype(jnp.float32) * MLA_SCALE
        sc = jnp.where(_chunk_mask(qp, k_pos), sc, -jnp.inf)
        pr = jax.nn.softmax(sc, axis=-1).astype(v_mla.dtype)
        return jnp.einsum('bhqk,bkhd->bqhd', pr, v_mla)

    o_mla = _sweep(mla_blk, q_pos, q_nope, q_pe).reshape(b, s, MLA_WIDTH)

    lam = (jnp.exp(jnp.sum((p['lambda_q1'][l] * p['lambda_k1'][l]).astype(jnp.float32)))
           - jnp.exp(jnp.sum((p['lambda_q2'][l] * p['lambda_k2'][l]).astype(jnp.float32)))
           + lam_init)
    rel_bias = p['rel_bias']

    def diff_blk(qp, qb):
        sc = jnp.einsum('bqhmd,bkhmd->bhmqk', qb, dk_all).astype(jnp.float32) * DIFF_SCALE
        bias = jnp.take(rel_bias, _rel_bucket(k_pos[None, :] - qp[:, None]), axis=0)
        sc = sc + jnp.transpose(bias, (2, 3, 0, 1)).astype(jnp.float32)
        sc = jnp.where(_chunk_mask(qp, k_pos), sc, -jnp.inf)
        pr = jax.nn.softmax(sc, axis=-1)
        wgt = pr[:, :, 0] - lam * pr[:, :, 1]
        return jnp.einsum('bhqk,bkhd->bqhd', wgt.astype(dv_all.dtype), dv_all)

    o_diff = _sweep(diff_blk, q_pos, dq)
    o_diff = _rmsnorm(o_diff, p['diff_subln'][l]) * (1.0 - lam_init)
    o = jnp.concatenate([o_mla, o_diff.reshape(b, s, DIFF_WIDTH)], axis=-1)
    return o, new_rows


def _swiglu(x, w_gu, w_down):
    g, u = jnp.split(x @ w_gu, 2, axis=-1)
    return (jax.nn.silu(g) * u) @ w_down


def _dispatch(x, idx, gates, w_gu, w_down):
    t, d = x.shape
    a = t * TOP_K
    flat_e = idx.reshape(a)
    order = jnp.argsort(flat_e)
    e_s = flat_e[order]
    t_s = (order // TOP_K).astype(jnp.int32)
    g_s = gates.reshape(a)[order]
    counts = jnp.bincount(flat_e, length=N_EXPERTS)
    padded = (counts + ROW_BLOCK - 1) // ROW_BLOCK * ROW_BLOCK
    pad_end = jnp.cumsum(padded)
    pad_start = pad_end - padded
    grp_start = jnp.cumsum(counts) - counts
    dest = pad_start[e_s] + jnp.arange(a) - grp_start[e_s]
    n_blocks = -(-a // ROW_BLOCK) + N_EXPERTS
    n_slots = n_blocks * ROW_BLOCK
    slot_tok = jnp.full((n_slots,), t, jnp.int32).at[dest].set(t_s)
    slot_gate = jnp.zeros((n_slots,), gates.dtype).at[dest].set(g_s)
    blk_e = jnp.minimum(jnp.searchsorted(pad_end, jnp.arange(n_blocks) * ROW_BLOCK, side='right'),
                        N_EXPERTS - 1)
    x_pad = jnp.concatenate([x, jnp.zeros((1, d), x.dtype)], axis=0)

    def run(args):
        tok, e = args
        return _swiglu(x_pad[tok], w_gu[e], w_down[e])

    rows = lax.map(run, (slot_tok.reshape(n_blocks, ROW_BLOCK), blk_e))
    rows = rows.reshape(n_slots, d) * slot_gate[:, None].astype(x.dtype)
    return jax.ops.segment_sum(rows, slot_tok, num_segments=t + 1)[:t]


def _moe(h, w_router, router_bias, w_gu, w_down, w_sh_gu, w_sh_down):
    b, s, d = h.shape
    x = h.reshape(b * s, d)
    t = x.shape[0]
    scores = jax.nn.sigmoid((x @ w_router).astype(jnp.float32))
    biased = scores + router_bias.astype(jnp.float32)
    grp_score = jnp.sum(lax.top_k(biased.reshape(t, N_GROUPS, -1), 2)[0], axis=-1)
    _, top_g = lax.top_k(grp_score, TOPK_GROUPS)
    gmask = jnp.any(top_g[..., None] == jnp.arange(N_GROUPS), axis=-2)
    masked = jnp.where(jnp.repeat(gmask, N_EXPERTS // N_GROUPS, axis=-1), biased, -jnp.inf)
    _, idx = lax.top_k(masked, TOP_K)
    sel = jnp.take_along_axis(scores, idx, axis=-1)
    gates = sel / jnp.sum(sel, axis=-1, keepdims=True) * ROUTE_SCALE
    routed = _dispatch(x, idx, gates, w_gu, w_down)
    shared = _swiglu(x, w_sh_gu, w_sh_down)
    return (routed + shared).reshape(b, s, d)


def _trunk(x, c, q_pos, past, p):
    rows = []
    for l in range(DEPTH):
        lam_init = 0.8 - 0.6 * math.exp(-0.3 * l)
        mod = jax.nn.silu(c) @ p['w_ada'][l] + p['b_ada'][l]
        sh_a, sc_a, g_a, sh_f, sc_f, g_f = jnp.split(mod[:, None, :], 6, axis=-1)
        h = _rmsnorm(x, p['norm_attn'][l]) * (1 + sc_a) + sh_a
        lay_past = None if past is None else tuple(a[l] for a in past)
        o, new = _mixers(h, q_pos, lay_past, p, l, lam_init)
        x = x + g_a * (o @ p['w_out'][l])
        h = _rmsnorm(x, p['norm_ffn'][l]) * (1 + sc_f) + sh_f
        x = x + g_f * _moe(h, p['w_router'][l], p['router_bias'][l], p['w_exp_gu'][l],
                           p['w_exp_down'][l], p['w_shared_gu'][l], p['w_shared_down'][l])
        rows.append(new)
    lat = jnp.stack([r[0] for r in rows])
    kpe = jnp.stack([r[1] for r in rows])
    dk = jnp.stack([r[2] for r in rows])
    dv = jnp.stack([r[3] for r in rows])
    return _rmsnorm(x, p['final_norm']), (lat, kpe, dk, dv)


def setup_inputs(seed: int = 0) -> dict:
    key = jax.random.key(seed)
    ks = iter(jax.random.split(key, 48))

    def nrm(shape, scale=1.0):
        return jax.random.normal(next(ks), shape, jnp.float32) * scale

    def gain(shape):
        return 1.0 + nrm(shape, 0.02)

    return {
        'x_prompt': nrm((BATCH, SEQ, D_MODEL)),
        'x_sample': nrm((DEC_BATCH, DEC_SEQ, D_MODEL)),
        'cache_mla_latent': nrm((DEPTH, DEC_BATCH, PAST_LEN, KV_LORA)),
        'cache_mla_krope': nrm((DEPTH, DEC_BATCH, PAST_LEN, MLA_ROPE)),
        'cache_diff_k': nrm((DEPTH, DEC_BATCH, PAST_LEN, DIFF_HEADS, 2, DIFF_DH)),
        'cache_diff_v': nrm((DEPTH, DEC_BATCH, PAST_LEN, DIFF_HEADS, DIFF_V)),
        'c_prompt': nrm((BATCH, D_MODEL)),
        'c_sample': nrm((DEC_BATCH, D_MODEL)),
        'w_ada': nrm((DEPTH, D_MODEL, 6 * D_MODEL), 0.5 * D_MODEL ** -0.5),
        'b_ada': nrm((DEPTH, 6 * D_MODEL), 0.02),
        'norm_attn': gain((DEPTH, D_MODEL)),
        'w_in': nrm((DEPTH, D_MODEL, IN_WIDTH), D_MODEL ** -0.5),
        'mla_q_norm': gain((DEPTH, Q_LORA)),
        'mla_kv_norm': gain((DEPTH, KV_LORA)),
        'w_uq': nrm((DEPTH, Q_LORA, MLA_HEADS * (MLA_NOPE + MLA_ROPE)), Q_LORA ** -0.5),
        'w_ukv': nrm((DEPTH, KV_LORA, MLA_HEADS * (MLA_NOPE + MLA_V)), KV_LORA ** -0.5),
        'lambda_q1': nrm((DEPTH, DIFF_DH), 0.1),
        'lambda_k1': nrm((DEPTH, DIFF_DH), 0.1),
        'lambda_q2': nrm((DEPTH, DIFF_DH), 0.1),
        'lambda_k2': nrm((DEPTH, DIFF_DH), 0.1),
        'diff_subln': gain((DEPTH, DIFF_V)),
        'rel_bias': nrm((NUM_BUCKETS, DIFF_HEADS, 2), 0.2),
        'w_out': nrm((DEPTH, MIX_WIDTH, D_MODEL), MIX_WIDTH ** -0.5),
        'norm_ffn': gain((DEPTH, D_MODEL)),
        'w_router': nrm((DEPTH, D_MODEL, N_EXPERTS), D_MODEL ** -0.5),
        'router_bias': nrm((DEPTH, N_EXPERTS), 0.01),
        'w_exp_gu': nrm((DEPTH, N_EXPERTS, D_MODEL, 2 * EXPERT_FF), D_MODEL ** -0.5),
        'w_exp_down': nrm((DEPTH, N_EXPERTS, EXPERT_FF, D_MODEL), EXPERT_FF ** -0.5),
        'w_shared_gu': nrm((DEPTH, D_MODEL, 2 * SHARED_FF), D_MODEL ** -0.5),
        'w_shared_down': nrm((DEPTH, SHARED_FF, D_MODEL), SHARED_FF ** -0.5),
        'final_norm': gain((D_MODEL,)),
    }


def reference(x_prompt, x_sample, cache_mla_latent, cache_mla_krope, cache_diff_k, cache_diff_v,
              c_prompt, c_sample, w_ada, b_ada, norm_attn, w_in, mla_q_norm, mla_kv_norm, w_uq, w_ukv,
              lambda_q1, lambda_k1, lambda_q2, lambda_k2, diff_subln, rel_bias, w_out, norm_ffn,
              w_router, router_bias, w_exp_gu, w_exp_down, w_shared_gu, w_shared_down, final_norm):
    p = dict(w_ada=w_ada, b_ada=b_ada, norm_attn=norm_attn, w_in=w_in, mla_q_norm=mla_q_norm,
             mla_kv_norm=mla_kv_norm, w_uq=w_uq, w_ukv=w_ukv, lambda_q1=lambda_q1, lambda_k1=lambda_k1,
             lambda_q2=lambda_q2, lambda_k2=lambda_k2, diff_subln=diff_subln, rel_bias=rel_bias,
             w_out=w_out, norm_ffn=norm_ffn, w_router=w_router, router_bias=router_bias,
             w_exp_gu=w_exp_gu, w_exp_down=w_exp_down, w_shared_gu=w_shared_gu,
             w_shared_down=w_shared_down, final_norm=final_norm)
    pos_p = jnp.arange(x_prompt.shape[1], dtype=jnp.int32)
    y_prompt, (lat_p, kpe_p, dk_p, dv_p) = _trunk(x_prompt, c_prompt, pos_p, None, p)
    past_len = cache_mla_latent.shape[2]
    pos_s = past_len + jnp.arange(x_sample.shape[1], dtype=jnp.int32)
    y_sample, (lat_s, kpe_s, dk_s, dv_s) = _trunk(
        x_sample, c_sample, pos_s, (cache_mla_latent, cache_mla_krope, cache_diff_k, cache_diff_v), p)
    return (y_prompt, y_sample, lat_p, kpe_p, dk_p, dv_p, lat_s, kpe_s, dk_s, dv_s)
```

```python
import functools
import math

import jax
import jax.numpy as jnp
import numpy as np
from jax import lax
from jax.experimental import pallas as pl
from jax.experimental.pallas import tpu as pltpu

F32 = jnp.float32
BF16 = jnp.bfloat16
NEG_INF = float("-inf")

CHUNK = 64
EPS = 1e-6
MLA_HEADS = 8
MLA_NOPE = 64
MLA_ROPE = 32
MLA_V = 64
ROPE_THETA = 10000.0
MLA_SCALE = (MLA_NOPE + MLA_ROPE) ** -0.5
DIFF_HEADS = 4
DIFF_DH = 64
DIFF_V = 2 * DIFF_DH
DIFF_SCALE = DIFF_DH ** -0.5
NUM_BUCKETS = 32
MAX_DISTANCE = 128
N_GROUPS = 8
TOPK_GROUPS = 4
TOP_K = 8
ROUTE_SCALE = 2.5

LANES = 128
HEAD_PAD = LANES
VMEM_LIMIT = 56 * 1024 * 1024

ROW_TILE = 256
ATTN_TILE = 512
EXPERT_BLOCK = 256


def _cparams(*sem):
    return pltpu.CompilerParams(dimension_semantics=sem, vmem_limit_bytes=VMEM_LIMIT)


def _nt(a, b):
    return lax.dot_general(a, b, (((1,), (1,)), ((), ())), preferred_element_type=F32)


def _mm(a, b):
    return jnp.dot(a, b, preferred_element_type=F32)


def _rms(x, g):
    return x * lax.rsqrt(jnp.mean(x * x, axis=-1, keepdims=True) + EPS) * g


def _pack_bf16_pairs(a):
    w = a.shape[1] // 2
    lo = lax.bitcast_convert_type(a[:, :w].astype(BF16).astype(F32), jnp.uint32)
    hi = lax.bitcast_convert_type(a[:, w:].astype(BF16).astype(F32), jnp.uint32)
    return (lo >> 16) | (hi & jnp.uint32(0xFFFF0000))


def _unpack_bf16_pairs(u):
    lo = lax.bitcast_convert_type(u << 16, F32)
    hi = lax.bitcast_convert_type(u & jnp.uint32(0xFFFF0000), F32)
    return lo, hi


def _ada_kernel(c_ref, w_ref, b_ref, o_ref):
    c = c_ref[...]
    s = (c * jax.nn.sigmoid(c)).astype(BF16)
    o_ref[...] = _mm(s, w_ref[...].astype(BF16)) + b_ref[...]


def _ada(c, w, b):
    n, d = c.shape
    cols = w.shape[1]
    tn = 1024
    return pl.pallas_call(
        _ada_kernel,
        grid=(cols // tn,),
        in_specs=[pl.BlockSpec((n, d), lambda j: (0, 0)),
                  pl.BlockSpec((d, tn), lambda j: (0, j)),
                  pl.BlockSpec((1, tn), lambda j: (0, j))],
        out_specs=pl.BlockSpec((n, tn), lambda j: (0, j)),
        out_shape=jax.ShapeDtypeStruct((n, cols), F32),
        compiler_params=_cparams("arbitrary"),
        name="ada_mod",
    )(c, w, b.reshape(1, cols))


def _pre_kernel(x_ref, sc_ref, sh_ref, g_ref, win_ref, gq_ref, gkv_ref, wq_ref, wkv_ref,
                ck_ref, sk_ref, cq_ref, sq_ref,
                lat_ref, kpe_ref, dk_ref, dv_ref, q_ref, k_ref, v_ref, dqb_ref, dkb_ref, dvb_ref,
                *, q_lora, kv_lora, diff_qk, diff_w):
    h = _rms(x_ref[...], g_ref[...]) * (1.0 + sc_ref[...]) + sh_ref[...]
    z = _mm(h.astype(BF16), win_ref[...])
    o = 0
    c_q = z[:, o:o + q_lora]; o += q_lora
    c_kv = z[:, o:o + kv_lora]; o += kv_lora
    dq = z[:, o:o + diff_qk]; o += diff_qk
    dk = z[:, o:o + diff_qk]; o += diff_qk
    dv = z[:, o:o + diff_w]; o += diff_w
    kpa = z[:, o:o + HEAD_PAD]; o += HEAD_PAD
    kps = z[:, o:o + HEAD_PAD]
    lat = _rms(c_kv, gkv_ref[...])
    kr = kpa * ck_ref[...] + kps * sk_ref[...]
    lat_ref[...] = lat
    kpe_ref[...] = kr[:, :MLA_ROPE]
    dk_ref[...] = dk
    dv_ref[...] = dv
    qn = _rms(c_q, gq_ref[...]).astype(BF16)
    zq = _mm(qn, wq_ref[...])
    qw = MLA_HEADS * HEAD_PAD
    cq = jnp.concatenate([cq_ref[...]] * MLA_HEADS, axis=1)
    sq = jnp.concatenate([sq_ref[...]] * MLA_HEADS, axis=1)
    q_ref[...] = (zq[:, :qw] * cq + zq[:, qw:] * sq).astype(BF16)
    zk = _mm(lat.astype(BF16), wkv_ref[...])
    k_ref[...] = (zk[:, :qw] + jnp.concatenate([kr] * MLA_HEADS, axis=1)).astype(BF16)
    v_ref[...] = zk[:, qw:].astype(BF16)
    dqb_ref[...] = (dq * DIFF_SCALE).astype(BF16)
    dkb_ref[...] = dk.astype(BF16)
    dvb_ref[...] = dv.astype(BF16)


def _pre(x, sc, sh, g, win, gq, gkv, wq, wkv, ck, sk, cq, sq, dims):
    nb, n, d = x.shape
    q_lora, kv_lora, diff_qk, diff_w = dims
    tm = min(ROW_TILE, n)
    mrows = sc.shape[1]
    mt = 1 if mrows == 1 else tm
    tok = lambda w: pl.BlockSpec((None, tm, w), lambda b, i: (b, i, 0))
    mod = pl.BlockSpec((None, mt, d), (lambda b, i: (b, 0, 0)) if mrows == 1 else (lambda b, i: (b, i, 0)))
    full = lambda a: pl.BlockSpec(a.shape, lambda b, i: (0,) * a.ndim)
    tab = pl.BlockSpec((tm, HEAD_PAD), lambda b, i: (i, 0))
    qw = MLA_HEADS * HEAD_PAD
    vw = MLA_HEADS * MLA_V
    widths = [(kv_lora, F32), (MLA_ROPE, F32), (diff_qk, F32), (diff_w, F32),
              (qw, BF16), (qw, BF16), (vw, BF16), (diff_qk, BF16), (diff_qk, BF16), (diff_w, BF16)]
    return pl.pallas_call(
        functools.partial(_pre_kernel, q_lora=q_lora, kv_lora=kv_lora, diff_qk=diff_qk, diff_w=diff_w),
        grid=(nb, n // tm),
        in_specs=[tok(d), mod, mod, full(g), full(win), full(gq), full(gkv), full(wq), full(wkv),
                  tab, tab, tab, tab],
        out_specs=[tok(w) for w, _ in widths],
        out_shape=[jax.ShapeDtypeStruct((nb, n, w), t) for w, t in widths],
        compiler_params=_cparams("parallel", "arbitrary"),
        name="pre_attn",
    )(x, sc, sh, g, win, gq, gkv, wq, wkv, ck, sk, cq, sq)


def _online_update(s, v, m_ref, l_ref, acc_ref):
    m_prev = m_ref[...]
    m_new = jnp.maximum(m_prev, jnp.max(s, axis=-1, keepdims=True))
    alpha = jnp.exp(m_prev - m_new)
    p = jnp.exp(s - m_new)
    l_ref[...] = alpha * l_ref[...] + jnp.sum(p, axis=-1, keepdims=True)
    acc_ref[...] = alpha * acc_ref[...] + _mm(p.astype(BF16), v)
    m_ref[...] = m_new


def _mla_kernel(q_ref, k_ref, v_ref, o_ref, m_ref, l_ref, acc_ref, *, tile):
    qi = pl.program_id(2)
    row = lax.broadcasted_iota(jnp.int32, (tile, tile), 0)
    col = lax.broadcasted_iota(jnp.int32, (tile, tile), 1)
    visible = (col // CHUNK) <= (row // CHUNK)
    outs = []
    for hh in range(2):
        lanes = slice(hh * HEAD_PAD, (hh + 1) * HEAD_PAD)
        q = q_ref[:, lanes]
        m_ref[...] = jnp.full(m_ref.shape, NEG_INF, F32)
        l_ref[...] = jnp.zeros(l_ref.shape, F32)
        acc_ref[...] = jnp.zeros(acc_ref.shape, F32)

        def step(j, mask, q=q, lanes=lanes):
            rows = pl.ds(pl.multiple_of(j * tile, tile), tile)
            s = _nt(q, k_ref[rows, lanes])
            if mask is not None:
                s = jnp.where(mask, s, NEG_INF)
            _online_update(s, v_ref[rows, :], m_ref, l_ref, acc_ref)

        def body(j, carry, step=step):
            step(j, None)
            return carry

        lax.fori_loop(0, qi, body, 0)
        step(qi, visible)
        outs.append(acc_ref[...] / l_ref[...])
    lane = lax.broadcasted_iota(jnp.int32, outs[0].shape, 1)
    o_ref[...] = jnp.where(lane < MLA_V, outs[0], outs[1]).astype(BF16)


def _mla_prompt(q, k, v):
    b, s, _ = q.shape
    tile = min(ATTN_TILE, s)
    pairs = MLA_HEADS // 2
    return pl.pallas_call(
        functools.partial(_mla_kernel, tile=tile),
        grid=(b, pairs, s // tile),
        in_specs=[pl.BlockSpec((None, tile, 2 * HEAD_PAD), lambda bi, h, i: (bi, i, h)),
                  pl.BlockSpec((None, s, 2 * HEAD_PAD), lambda bi, h, i: (bi, 0, h)),
                  pl.BlockSpec((None, s, 2 * MLA_V), lambda bi, h, i: (bi, 0, h))],
        out_specs=pl.BlockSpec((None, tile, 2 * MLA_V), lambda bi, h, i: (bi, i, h)),
        out_shape=jax.ShapeDtypeStruct((b, s, MLA_HEADS * MLA_V), BF16),
        scratch_shapes=[pltpu.VMEM((tile, 1), F32), pltpu.VMEM((tile, 1), F32),
                        pltpu.VMEM((tile, 2 * MLA_V), F32)],
        compiler_params=_cparams("parallel", "parallel", "arbitrary"),
        name="mla_prompt",
    )(q, k, v)


def _lambda(lq1, lk1, lq2, lk2, lam_init):
    return (jnp.exp(jnp.sum(lq1[...] * lk1[...], keepdims=True))
            - jnp.exp(jnp.sum(lq2[...] * lk2[...], keepdims=True)) + lam_init)


def _diff_finish(o0, o1, lam, subln, lam_init):
    return (_rms(o0 - lam * o1, subln) * (1.0 - lam_init)).astype(BF16)


def _diff_kernel(far_ref, q_ref, k_ref, v_ref, bias_ref, lq1, lk1, lq2, lk2, subln_ref, o_ref,
                 m_ref, l_ref, acc_ref, *, tile, lam_init):
    h = pl.program_id(1)
    qi = pl.program_id(2)
    q = q_ref[...]
    lane = lax.broadcasted_iota(jnp.int32, q.shape, 1)
    zero = jnp.zeros_like(q)
    qm = (jnp.where(lane < DIFF_DH, q, zero), jnp.where(lane >= DIFF_DH, q, zero))
    m_ref[...] = jnp.full(m_ref.shape, NEG_INF, F32)
    l_ref[...] = jnp.zeros(l_ref.shape, F32)
    acc_ref[...] = jnp.zeros(acc_ref.shape, F32)

    def step(j, which):
        rows = pl.ds(pl.multiple_of(j * tile, tile), tile)
        k = k_ref[rows, :]
        v = v_ref[rows, :]
        for m in range(2):
            s = _nt(qm[m], k)
            s = s + (far_ref[h, m] if which is None else bias_ref[m, which])
            _online_update(s, v, m_ref.at[m], l_ref.at[m], acc_ref.at[m])

    def body(j, carry):
        step(j, None)
        return carry

    lax.fori_loop(0, jnp.maximum(qi - 1, 0), body, 0)

    @pl.when(qi >= 1)
    def _():
        step(qi - 1, 1)

    step(qi, 0)
    lam = _lambda(lq1, lk1, lq2, lk2, lam_init)
    o_ref[...] = _diff_finish(acc_ref[0] / l_ref[0], acc_ref[1] / l_ref[1], lam, subln_ref[...], lam_init)


def _diff_prompt(far, q, k, v, bias, lq1, lk1, lq2, lk2, subln, lam_init):
    b, s, _ = q.shape
    tile = bias.shape[-1]
    vec = lambda a: pl.BlockSpec(a.shape, lambda bi, h, i: (0, 0))
    return pl.pallas_call(
        functools.partial(_diff_kernel, tile=tile, lam_init=lam_init),
        grid=(b, DIFF_HEADS, s // tile),
        in_specs=[pl.BlockSpec(memory_space=pltpu.SMEM),
                  pl.BlockSpec((None, tile, DIFF_V), lambda bi, h, i: (bi, i, h)),
                  pl.BlockSpec((None, s, DIFF_V), lambda bi, h, i: (bi, 0, h)),
                  pl.BlockSpec((None, s, DIFF_V), lambda bi, h, i: (bi, 0, h)),
                  pl.BlockSpec((None, 2, 2, tile, tile), lambda bi, h, i: (h, 0, 0, 0, 0)),
                  vec(lq1), vec(lk1), vec(lq2), vec(lk2), vec(subln)],
        out_specs=pl.BlockSpec((None, tile, DIFF_V), lambda bi, h, i: (bi, i, h)),
        out_shape=jax.ShapeDtypeStruct((b, s, DIFF_HEADS * DIFF_V), BF16),
        scratch_shapes=[pltpu.VMEM((2, tile, 1), F32), pltpu.VMEM((2, tile, 1), F32),
                        pltpu.VMEM((2, tile, DIFF_V), F32)],
        compiler_params=_cparams("parallel", "parallel", "arbitrary"),
        name="diff_prompt",
    )(far, q, k, v, bias, lq1, lk1, lq2, lk2, subln)


def _softmax2(s_c, s_n):
    m = jnp.maximum(jnp.max(s_c, axis=-1, keepdims=True), jnp.max(s_n, axis=-1, keepdims=True))
    p_c = jnp.exp(s_c - m)
    p_n = jnp.exp(s_n - m)
    l = jnp.sum(p_c, axis=-1, keepdims=True) + jnp.sum(p_n, axis=-1, keepdims=True)
    return p_c.astype(BF16), p_n.astype(BF16), l


def _mla_sample_kernel(q_ref, latc_ref, kpec_ref, latn_ref, kpen_ref, wk_ref, wv_ref, mask_ref, o_ref):
    q = q_ref[...]
    latc = latc_ref[...].astype(BF16)
    kpec = kpec_ref[...].astype(BF16)
    latn = latn_ref[...].astype(BF16)
    kpen = kpen_ref[...].astype(BF16)
    qlat, qrope = [], []
    for h in range(MLA_HEADS):
        lanes = slice(h * HEAD_PAD, (h + 1) * HEAD_PAD)
        qh = q[:, lanes]
        qlat.append(_nt(qh, wk_ref[:, lanes]).astype(BF16))
        qrope.append(qh[:, :MLA_ROPE])
    qlat = jnp.concatenate(qlat, axis=0)
    qrope = jnp.concatenate(qrope, axis=0)
    s_c = _nt(qlat, latc) + _nt(qrope, kpec)
    s_n = _nt(qlat, latn) + _nt(qrope, kpen) + jnp.concatenate([mask_ref[...]] * MLA_HEADS, axis=0)
    p_c, p_n, l = _softmax2(s_c, s_n)
    o_lat = ((_mm(p_c, latc) + _mm(p_n, latn)) / l).astype(BF16)
    n = q.shape[0]
    lane = lax.broadcasted_iota(jnp.int32, (n, MLA_HEADS * MLA_V), 1)
    out = jnp.zeros((n, MLA_HEADS * MLA_V), F32)
    for h in range(MLA_HEADS):
        oh = _mm(o_lat[h * n:(h + 1) * n], wv_ref[...])
        out = jnp.where(lane // MLA_V == h, oh, out)
    o_ref[...] = out.astype(BF16)


def _mla_sample(q, latc, kpec, latn, kpen, wk, wv, mask):
    bd, n, _ = q.shape
    p = latc.shape[1]
    per = lambda a: pl.BlockSpec((None,) + a.shape[1:], lambda b: (b, 0, 0))
    full = lambda a: pl.BlockSpec(a.shape, lambda b: (0, 0))
    return pl.pallas_call(
        _mla_sample_kernel,
        grid=(bd,),
        in_specs=[per(q), per(latc), per(kpec), per(latn), per(kpen), full(wk), full(wv), full(mask)],
        out_specs=pl.BlockSpec((None, n, MLA_HEADS * MLA_V), lambda b: (b, 0, 0)),
        out_shape=jax.ShapeDtypeStruct((bd, n, MLA_HEADS * MLA_V), BF16),
        compiler_params=_cparams("parallel"),
        name="mla_sample",
    )(q, latc, kpec, latn, kpen, wk, wv, mask)


def _diff_sample_kernel(q_ref, kc_ref, vc_ref, kn_ref, vn_ref, bc_ref, bn_ref, lq1, lk1, lq2, lk2, subln_ref,
                        o_ref, *, lam_init):
    q = q_ref[...]
    lane = lax.broadcasted_iota(jnp.int32, q.shape, 1)
    zero = jnp.zeros_like(q)
    qm = (jnp.where(lane < DIFF_DH, q, zero), jnp.where(lane >= DIFF_DH, q, zero))
    kc = kc_ref[...].astype(BF16)
    vc = vc_ref[...].astype(BF16)
    kn = kn_ref[...]
    vn = vn_ref[...]
    outs = []
    for m in range(2):
        s_c = _nt(qm[m], kc) + bc_ref[m]
        s_n = _nt(qm[m], kn) + bn_ref[m]
        p_c, p_n, l = _softmax2(s_c, s_n)
        outs.append((_mm(p_c, vc) + _mm(p_n, vn)) / l)
    lam = _lambda(lq1, lk1, lq2, lk2, lam_init)
    o_ref[...] = _diff_finish(outs[0], outs[1], lam, subln_ref[...], lam_init)


def _diff_sample(q, kc, vc, kn, vn, bias_c, bias_n, lq1, lk1, lq2, lk2, subln, lam_init):
    bd, n, _ = q.shape
    p = kc.shape[1]
    vec = lambda a: pl.BlockSpec(a.shape, lambda b, h: (0, 0))
    head = lambda rows: pl.BlockSpec((None, rows, DIFF_V), lambda b, h: (b, 0, h))
    return pl.pallas_call(
        functools.partial(_diff_sample_kernel, lam_init=lam_init),
        grid=(bd, DIFF_HEADS),
        in_specs=[head(n), head(p), head(p), head(n), head(n),
                  pl.BlockSpec((None, 2, n, p), lambda b, h: (h, 0, 0, 0)),
                  pl.BlockSpec((None, 2, n, n), lambda b, h: (h, 0, 0, 0)),
                  vec(lq1), vec(lk1), vec(lq2), vec(lk2), vec(subln)],
        out_specs=head(n),
        out_shape=jax.ShapeDtypeStruct((bd, n, DIFF_HEADS * DIFF_V), BF16),
        compiler_params=_cparams("parallel", "arbitrary"),
        name="diff_sample",
    )(q, kc, vc, kn, vn, bias_c, bias_n, lq1, lk1, lq2, lk2, subln)


def _route(logits, rb):
    n_exp, n = logits.shape
    gsz = n_exp // N_GROUPS
    scores = jax.nn.sigmoid(logits)
    biased = scores + rb
    io_g = lax.broadcasted_iota(jnp.int32, (gsz, n), 0)
    gscore = []
    for g in range(N_GROUPS):
        v = biased[g * gsz:(g + 1) * gsz]
        m1 = jnp.max(v, axis=0, keepdims=True)
        i1 = jnp.min(jnp.where(v == m1, io_g, gsz), axis=0, keepdims=True)
        m2 = jnp.max(jnp.where(io_g == i1, NEG_INF, v), axis=0, keepdims=True)
        gscore.append(m1 + m2)
    gscore = jnp.concatenate(gscore, axis=0)
    io_n = lax.broadcasted_iota(jnp.int32, (N_GROUPS, n), 0)
    chosen = jnp.zeros((N_GROUPS, n), jnp.int32)
    for _ in range(TOPK_GROUPS):
        m = jnp.max(gscore, axis=0, keepdims=True)
        i = jnp.min(jnp.where(gscore == m, io_n, N_GROUPS), axis=0, keepdims=True)
        hit = io_n == i
        chosen = jnp.where(hit, 1, chosen)
        gscore = jnp.where(hit, NEG_INF, gscore)
    masked = jnp.concatenate(
        [jnp.where(chosen[g:g + 1] > 0, biased[g * gsz:(g + 1) * gsz], NEG_INF) for g in range(N_GROUPS)], axis=0)
    io_e = lax.broadcasted_iota(jnp.int32, (n_exp, n), 0)
    idx, sel = [], []
    for _ in range(TOP_K):
        m = jnp.max(masked, axis=0, keepdims=True)
        i = jnp.min(jnp.where(masked == m, io_e, n_exp), axis=0, keepdims=True)
        hit = io_e == i
        idx.append(i)
        sel.append(jnp.sum(jnp.where(hit, scores, 0.0), axis=0, keepdims=True))
        masked = jnp.where(hit, NEG_INF, masked)
    idx = jnp.concatenate(idx, axis=0)
    sel = jnp.concatenate(sel, axis=0)
    gates = sel / jnp.sum(sel, axis=0, keepdims=True) * ROUTE_SCALE
    return idx, gates


def _post_kernel(x_ref, om_ref, od_ref, ga_ref, scf_ref, shf_ref, gf_ref, nf_ref, wo_ref, wrh_ref, wrl_ref,
                 rb_ref, wsg_ref, wsd_ref, base_ref, h2p_ref, idx_ref, gate_ref):
    mw = om_ref.shape[1]
    attn = _mm(om_ref[...], wo_ref[:mw, :]) + _mm(od_ref[...], wo_ref[mw:, :])
    x1 = x_ref[...] + ga_ref[...] * attn
    h2 = _rms(x1, nf_ref[...]) * (1.0 + scf_ref[...]) + shf_ref[...]
    hb = h2.astype(BF16)
    hlo = (h2 - hb.astype(F32)).astype(BF16)
    logits = _nt(wrh_ref[...], hb) + _nt(wrh_ref[...], hlo) + _nt(wrl_ref[...], hb)
    idx, gates = _route(logits, rb_ref[...])
    idx_ref[...] = idx
    gate_ref[...] = gates
    gu = _mm(hb, wsg_ref[...])
    ff = gu.shape[1] // 2
    g, u = gu[:, :ff], gu[:, ff:]
    shared = _mm((g * jax.nn.sigmoid(g) * u).astype(BF16), wsd_ref[...])
    base_ref[...] = x1 + gf_ref[...] * shared
    h2p_ref[...] = _pack_bf16_pairs(h2)


def _post(x, om, od, ga, scf, shf, gf, nf, wo, wrh, wrl, rb, wsg, wsd):
    nb, n, d = x.shape
    tm = min(ROW_TILE, n)
    mrows = ga.shape[1]
    mt = 1 if mrows == 1 else tm
    tok = lambda w: pl.BlockSpec((None, tm, w), lambda b, i: (b, i, 0))
    mod = pl.BlockSpec((None, mt, d), (lambda b, i: (b, 0, 0)) if mrows == 1 else (lambda b, i: (b, i, 0)))
    full = lambda a: pl.BlockSpec(a.shape, lambda b, i: (0,) * a.ndim)
    kt = pl.BlockSpec((None, TOP_K, tm), lambda b, i: (b, 0, i))
    return pl.pallas_call(
        _post_kernel,
        grid=(nb, n // tm),
        in_specs=[tok(d), tok(om.shape[2]), tok(od.shape[2]), mod, mod, mod, mod, full(nf), full(wo),
                  full(wrh), full(wrl), full(rb), full(wsg), full(wsd)],
        out_specs=[tok(d), tok(d // 2), kt, kt],
        out_shape=[jax.ShapeDtypeStruct((nb, n, d), F32), jax.ShapeDtypeStruct((nb, n, d // 2), jnp.uint32),
                   jax.ShapeDtypeStruct((nb, TOP_K, n), jnp.int32), jax.ShapeDtypeStruct((nb, TOP_K, n), F32)],
        compiler_params=_cparams("parallel", "arbitrary"),
        name="post_attn",
    )(x, om, od, ga, scf, shf, gf, nf, wo, wrh, wrl, rb, wsg, wsd)


def _row_copy_wait(src_ref, dst_ref, sem, rows):
    pltpu.make_async_copy(src_ref.at[pl.ds(0, rows)], dst_ref.at[pl.ds(0, rows)], sem).wait()


def _dispatch_kernel(dest_ref, h_ref, xs_in_ref, xs_ref, sem):
    del xs_in_ref
    n = h_ref.shape[0]

    def body(r, carry):
        for k in range(TOP_K):
            pltpu.make_async_copy(h_ref.at[pl.ds(r, 1)], xs_ref.at[pl.ds(dest_ref[k, r], 1)], sem).start()
        return carry

    lax.fori_loop(0, n, body, 0)
    for _ in range(TOP_K):
        _row_copy_wait(h_ref, xs_ref, sem, n)


def _dispatch(dest, h2p, xs):
    tiles, _, rows = dest.shape
    w = h2p.shape[1]
    return pl.pallas_call(
        _dispatch_kernel,
        grid=(tiles,),
        in_specs=[pl.BlockSpec((None, TOP_K, rows), lambda i: (i, 0, 0), memory_space=pltpu.SMEM),
                  pl.BlockSpec((rows, w), lambda i: (i, 0)),
                  pl.BlockSpec(memory_space=pl.ANY)],
        out_specs=pl.BlockSpec(memory_space=pl.ANY),
        out_shape=jax.ShapeDtypeStruct(xs.shape, xs.dtype),
        scratch_shapes=[pltpu.SemaphoreType.DMA(())],
        input_output_aliases={2: 0},
        compiler_params=_cparams("arbitrary"),
        name="moe_dispatch",
    )(dest, h2p, xs)


def _expert_kernel(blk_e_ref, used_ref, xs_ref, wgu_ref, wd_ref, ys_ref, wgu_bf, wd_bf):
    i = pl.program_id(0)
    prev = blk_e_ref[jnp.maximum(i - 1, 0)]

    @pl.when((i == 0) | (blk_e_ref[i] != prev))
    def _():
        wgu_bf[...] = wgu_ref[...].astype(BF16)
        wd_bf[...] = wd_ref[...].astype(BF16)

    @pl.when(i < used_ref[0])
    def _():
        lo, hi = _unpack_bf16_pairs(xs_ref[...])
        w = lo.shape[1]
        gu = _mm(lo.astype(BF16), wgu_bf[:w, :]) + _mm(hi.astype(BF16), wgu_bf[w:, :])
        ff = gu.shape[1] // 2
        g, u = gu[:, :ff], gu[:, ff:]
        y = _mm((g * jax.nn.sigmoid(g) * u).astype(BF16), wd_bf[...])
        ys_ref[...] = _pack_bf16_pairs(y)

    @pl.when(i >= used_ref[0])
    def _():
        ys_ref[...] = jnp.zeros(ys_ref.shape, ys_ref.dtype)


def _experts(blk_e, used, xs, w_gu, w_down):
    n_slots, w = xs.shape
    bm = EXPERT_BLOCK
    _, d, ff2 = w_gu.shape
    grid_spec = pltpu.PrefetchScalarGridSpec(
        num_scalar_prefetch=2,
        grid=(n_slots // bm,),
        in_specs=[pl.BlockSpec((bm, w), lambda i, be, nu: (i, 0)),
                  pl.BlockSpec((None, d, ff2), lambda i, be, nu: (be[i], 0, 0)),
                  pl.BlockSpec((None, ff2 // 2, d), lambda i, be, nu: (be[i], 0, 0))],
        out_specs=pl.BlockSpec((bm, w), lambda i, be, nu: (i, 0)),
        scratch_shapes=[pltpu.VMEM((d, ff2), BF16), pltpu.VMEM((ff2 // 2, d), BF16)],
    )
    return pl.pallas_call(
        _expert_kernel,
        grid_spec=grid_spec,
        out_shape=jax.ShapeDtypeStruct((n_slots, w), jnp.uint32),
        compiler_params=_cparams("arbitrary"),
        name="moe_experts",
    )(blk_e, used, xs, w_gu, w_down)


def _combine_kernel(dest_ref, gate_ref, base_ref, gf_ref, fn_ref, ys_ref, o_ref, buf, sem):
    n = base_ref.shape[0]

    def body(r, carry):
        for k in range(TOP_K):
            pltpu.make_async_copy(ys_ref.at[pl.ds(dest_ref[k, r], 1)], buf.at[k, pl.ds(r, 1)], sem).start()
        return carry

    lax.fori_loop(0, n, body, 0)
    for k in range(TOP_K):
        _row_copy_wait(ys_ref, buf.at[k], sem, n)
    gates = gate_ref[...]
    lo_acc = None
    for k in range(TOP_K):
        lo, hi = _unpack_bf16_pairs(buf[k])
        g = gates[:, k:k + 1]
        lo_acc = g * lo if lo_acc is None else lo_acc + g * lo
        hi_acc = g * hi if k == 0 else hi_acc + g * hi
    routed = jnp.concatenate([lo_acc, hi_acc], axis=1)
    o_ref[...] = _rms(base_ref[...] + gf_ref[...] * routed, fn_ref[...])


def _combine(dest, gates, base, gf, fn, ys):
    nb, n, d = base.shape
    rows = dest.shape[-1]
    mrows = gf.shape[1]
    mod = pl.BlockSpec((None, 1 if mrows == 1 else rows, d),
                       (lambda b, i: (b, 0, 0)) if mrows == 1 else (lambda b, i: (b, i, 0)))
    return pl.pallas_call(
        _combine_kernel,
        grid=(nb, n // rows),
        in_specs=[pl.BlockSpec((None, None, TOP_K, rows), lambda b, i: (b, i, 0, 0), memory_space=pltpu.SMEM),
                  pl.BlockSpec((None, rows, TOP_K), lambda b, i: (b, i, 0)),
                  pl.BlockSpec((None, rows, d), lambda b, i: (b, i, 0)),
                  mod,
                  pl.BlockSpec(fn.shape, lambda b, i: (0, 0)),
                  pl.BlockSpec(memory_space=pl.ANY)],
        out_specs=pl.BlockSpec((None, rows, d), lambda b, i: (b, i, 0)),
        out_shape=jax.ShapeDtypeStruct((nb, n, d), F32),
        scratch_shapes=[pltpu.VMEM((TOP_K, rows, d // 2), jnp.uint32), pltpu.SemaphoreType.DMA(())],
        compiler_params=_cparams("arbitrary", "arbitrary"),
        name="moe_combine",
    )(dest, gates, base, gf, fn, ys)


def _dispatch_plan(idx_flat, n_exp, bm):
    a = idx_flat.shape[0]
    order = jnp.argsort(idx_flat)
    e_s = idx_flat[order]
    grp_start = jnp.searchsorted(e_s, jnp.arange(n_exp, dtype=e_s.dtype), side="left").astype(jnp.int32)
    grp_end = jnp.concatenate([grp_start[1:], jnp.array([a], jnp.int32)])
    counts = grp_end - grp_start
    padded = (counts + bm - 1) // bm * bm
    pad_end = jnp.cumsum(padded)
    pad_start = pad_end - padded
    dest_sorted = pad_start[e_s] + jnp.arange(a, dtype=jnp.int32) - grp_start[e_s]
    dest = jnp.zeros((a,), jnp.int32).at[order].set(dest_sorted, unique_indices=True)
    n_blocks = -(-a // bm) + n_exp
    blk_e = jnp.minimum(jnp.searchsorted(pad_end, jnp.arange(n_blocks, dtype=jnp.int32) * bm, side="right"),
                        n_exp - 1).astype(jnp.int32)
    used = (pad_end[-1] // bm).astype(jnp.int32).reshape(1)
    return dest, blk_e, used, n_blocks


def _rel_bucket(rel):
    nb = NUM_BUCKETS // 2
    max_exact = nb // 2
    n = jnp.abs(rel)
    nf = jnp.maximum(n, 1).astype(F32)
    large = max_exact + (jnp.log(nf / max_exact) / math.log(MAX_DISTANCE / max_exact)
                         * (nb - max_exact)).astype(jnp.int32)
    large = jnp.minimum(large, nb - 1)
    return jnp.where(rel > 0, nb, 0) + jnp.where(n < max_exact, n, large)


def _bias_table(rel_bias, q_pos, k_pos, masked):
    rel = jnp.asarray(k_pos, jnp.int32)[None, :] - jnp.asarray(q_pos, jnp.int32)[:, None]
    bias = jnp.transpose(jnp.take(rel_bias, _rel_bucket(rel), axis=0), (2, 3, 0, 1)).astype(F32)
    if masked:
        vis = (np.asarray(k_pos)[None, :] // CHUNK) <= (np.asarray(q_pos)[:, None] // CHUNK)
        bias = jnp.where(jnp.asarray(vis), bias, NEG_INF)
    return bias


def _rope_tables(pos):
    half = MLA_ROPE // 2
    inv = ROPE_THETA ** (-jnp.arange(half, dtype=F32) / half)
    ang = pos.astype(F32)[:, None] * inv
    cos, sin = jnp.cos(ang), jnp.sin(ang)
    n = pos.shape[0]
    z = lambda w: jnp.zeros((n, w), F32)
    cc = jnp.concatenate([cos, cos], axis=1)
    ss = jnp.concatenate([-sin, sin], axis=1)
    ck = jnp.concatenate([cc, z(HEAD_PAD - MLA_ROPE)], axis=1)
    sk = jnp.concatenate([ss, z(HEAD_PAD - MLA_ROPE)], axis=1)
    cq = MLA_SCALE * jnp.concatenate([cc, jnp.ones((n, MLA_NOPE), F32), z(HEAD_PAD - MLA_ROPE - MLA_NOPE)], axis=1)
    sq = MLA_SCALE * sk
    return ck, sk, cq, sq


def _layer_weights(w_in, w_uq, w_ukv, q_lora, kv_lora, diff_qk, diff_w):
    d = w_in.shape[0]
    half = MLA_ROPE // 2
    swap = np.concatenate([np.arange(half, MLA_ROPE), np.arange(half)])
    o = q_lora + kv_lora
    w_kpe = w_in[:, o:o + MLA_ROPE]
    rest = w_in[:, o + MLA_ROPE:]
    zpad = jnp.zeros((d, HEAD_PAD - MLA_ROPE), w_in.dtype)
    win = jnp.concatenate([w_in[:, :o], rest, w_kpe, zpad, w_kpe[:, swap], zpad], axis=1).astype(BF16)
    r = w_uq.shape[0]
    wq3 = w_uq.reshape(r, MLA_HEADS, MLA_NOPE + MLA_ROPE)
    nope, rope = wq3[..., :MLA_NOPE], wq3[..., MLA_NOPE:]
    z3 = lambda w: jnp.zeros((r, MLA_HEADS, w), w_uq.dtype)
    wq_a = jnp.concatenate([rope, nope, z3(HEAD_PAD - MLA_ROPE - MLA_NOPE)], axis=-1)
    wq_s = jnp.concatenate([rope[..., swap], z3(HEAD_PAD - MLA_ROPE)], axis=-1)
    wq = jnp.concatenate([wq_a.reshape(r, -1), wq_s.reshape(r, -1)], axis=1).astype(BF16)
    rk = w_ukv.shape[0]
    wkv3 = w_ukv.reshape(rk, MLA_HEADS, MLA_NOPE + MLA_V)
    zk = lambda w: jnp.zeros((rk, MLA_HEADS, w), w_ukv.dtype)
    wk = jnp.concatenate([zk(MLA_ROPE), wkv3[..., :MLA_NOPE], zk(HEAD_PAD - MLA_ROPE - MLA_NOPE)], axis=-1)
    wk = wk.reshape(rk, -1).astype(BF16)
    wv = wkv3[..., MLA_NOPE:].reshape(rk, -1).astype(BF16)
    return win, wq, wk, wv


def kernel(x_prompt, x_sample, cache_mla_latent, cache_mla_krope, cache_diff_k, cache_diff_v, c_prompt, c_sample, w_ada, b_ada, norm_attn, w_in, mla_q_norm, mla_kv_norm, w_uq, w_ukv, lambda_q1, lambda_k1, lambda_q2, lambda_k2, diff_subln, rel_bias, w_out, norm_ffn, w_router, router_bias, w_exp_gu, w_exp_down, w_shared_gu, w_shared_down, final_norm):
    b, s, d = x_prompt.shape
    bd, sd, _ = x_sample.shape
    depth = w_ada.shape[0]
    past = cache_mla_latent.shape[2]
    q_lora = mla_q_norm.shape[1]
    kv_lora = mla_kv_norm.shape[1]
    diff_qk = DIFF_HEADS * 2 * DIFF_DH
    diff_w = DIFF_HEADS * DIFF_V
    dims = (q_lora, kv_lora, diff_qk, diff_w)
    n_exp = w_router.shape[2]
    ns = bd * sd
    tile = min(ATTN_TILE, s)
    assert s % tile == 0 and tile % CHUNK == 0 and tile >= MAX_DISTANCE

    pos_p = np.arange(s)
    pos_s = past + np.arange(sd)
    tabs_p = _rope_tables(jnp.asarray(pos_p, jnp.int32))
    tabs_s = tuple(jnp.tile(t, (bd, 1)) for t in _rope_tables(jnp.asarray(pos_s, jnp.int32)))
    bias_p = jnp.stack([_bias_table(rel_bias, pos_p[:tile], pos_p[:tile], True),
                        _bias_table(rel_bias, pos_p[tile:2 * tile] if s > tile else pos_p[:tile] + tile,
                                    pos_p[:tile], False)], axis=2)
    far_p = _bias_table(rel_bias, np.array([2 * tile]), np.array([0]), False)[:, :, 0, 0]
    bias_sc = _bias_table(rel_bias, pos_s, np.arange(past), True)
    bias_sn = _bias_table(rel_bias, pos_s, pos_s, True)
    mask_sn = jnp.asarray(np.where((pos_s[None, :] // CHUNK) <= (pos_s[:, None] // CHUNK), 0.0, NEG_INF), F32)
    assert bool(np.all((np.arange(past)[None, :] // CHUNK) <= (pos_s[:, None] // CHUNK)))

    xp, xs_tok = x_prompt, x_sample.reshape(1, ns, d)
    c_all = jnp.concatenate([c_prompt, c_sample], axis=0)
    row2 = lambda a: a.reshape(1, -1)
    new_p, new_s = [], []
    for l in range(depth):
        lam_init = 0.8 - 0.6 * math.exp(-0.3 * l)
        mod = _ada(c_all, w_ada[l], b_ada[l])
        mods = jnp.split(mod, 6, axis=-1)
        mod_p = [m[:b].reshape(b, 1, d) for m in mods]
        mod_s = [jnp.repeat(m[b:], sd, axis=0).reshape(1, ns, d) for m in mods]
        win, wq, wk, wv = _layer_weights(w_in[l], w_uq[l], w_ukv[l], *dims)
        wkv = jnp.concatenate([wk, wv], axis=1)
        lam_vecs = (row2(lambda_q1[l]), row2(lambda_k1[l]), row2(lambda_q2[l]), row2(lambda_k2[l]))
        subln = row2(diff_subln[l])
        shared_pre = (row2(norm_attn[l]), win, row2(mla_q_norm[l]), row2(mla_kv_norm[l]), wq, wkv)

        lat, kpe, dk, dv, q, k, v, dqb, dkb, dvb = _pre(xp, mod_p[1], mod_p[0], *shared_pre, *tabs_p, dims)
        new_p.append((lat, kpe, dk, dv))
        om_p = _mla_prompt(q, k, v)
        od_p = _diff_prompt(far_p, dqb, dkb, dvb, bias_p, *lam_vecs, subln, lam_init)

        lat_s, kpe_s, dk_s, dv_s, q_s, _, _, dqb_s, dkb_s, dvb_s = _pre(
            xs_tok, mod_s[1], mod_s[0], *shared_pre, *tabs_s, dims)
        new_s.append((lat_s, kpe_s, dk_s, dv_s))
        per = lambda a: a.reshape(bd, sd, a.shape[-1])
        om_s = _mla_sample(per(q_s), cache_mla_latent[l], cache_mla_krope[l], per(lat_s), per(kpe_s), wk, wv,
                           mask_sn)
        od_s = _diff_sample(per(dqb_s), cache_diff_k[l].reshape(bd, past, diff_qk),
                            cache_diff_v[l].reshape(bd, past, diff_w), per(dkb_s), per(dvb_s),
                            bias_sc, bias_sn, *lam_vecs, subln, lam_init)
        om_s = om_s.reshape(1, ns, -1)
        od_s = od_s.reshape(1, ns, -1)

        wr_t = w_router[l].T
        wrh = wr_t.astype(BF16)
        wrl = (wr_t - wrh.astype(F32)).astype(BF16)
        post_w = (row2(norm_ffn[l]), w_out[l].astype(BF16), wrh, wrl, router_bias[l].reshape(n_exp, 1),
                  w_shared_gu[l].astype(BF16), w_shared_down[l].astype(BF16))
        base_p, h2p_p, idx_p, gate_p = _post(xp, om_p, od_p, mod_p[2], mod_p[4], mod_p[3], mod_p[5], *post_w)
        base_s, h2p_s, idx_s, gate_s = _post(xs_tok, om_s, od_s, mod_s[2], mod_s[4], mod_s[3], mod_s[5], *post_w)

        tp = b * s
        idx_all = jnp.concatenate([jnp.transpose(idx_p, (1, 0, 2)).reshape(TOP_K, tp), idx_s[0]], axis=1)
        dest, blk_e, used, n_blocks = _dispatch_plan(idx_all.reshape(-1), n_exp, EXPERT_BLOCK)
        dest = dest.reshape(TOP_K, tp + ns)
        rows_p = min(ROW_TILE, s)
        rows_s = min(ROW_TILE, ns)
        dest_p = jnp.transpose(dest[:, :tp].reshape(TOP_K, tp // rows_p, rows_p), (1, 0, 2))
        dest_s = jnp.transpose(dest[:, tp:].reshape(TOP_K, ns // rows_s, rows_s), (1, 0, 2))
        slots = jnp.zeros((n_blocks * EXPERT_BLOCK, d // 2), jnp.uint32)
        slots = _dispatch(dest_p, h2p_p.reshape(tp, d // 2), slots)
        slots = _dispatch(dest_s, h2p_s.reshape(ns, d // 2), slots)
        ys = _experts(blk_e, used, slots, w_exp_gu[l], w_exp_down[l])
        fn = row2(final_norm) if l == depth - 1 else jnp.ones((1, d), F32)
        assert depth == 1
        xp = _combine(dest_p.reshape(b, s // rows_p, TOP_K, rows_p), jnp.transpose(gate_p, (0, 2, 1)),
                      base_p, mod_p[5], fn, ys)
        xs_tok = _combine(dest_s.reshape(1, ns // rows_s, TOP_K, rows_s), jnp.transpose(gate_s, (0, 2, 1)),
                          base_s, mod_s[5], fn, ys)

    stack = lambda rows, i, shape: jnp.stack([r[i] for r in rows]).reshape(shape)
    outs_p = (stack(new_p, 0, (depth, b, s, kv_lora)), stack(new_p, 1, (depth, b, s, MLA_ROPE)),
              stack(new_p, 2, (depth, b, s, DIFF_HEADS, 2, DIFF_DH)), stack(new_p, 3, (depth, b, s, DIFF_HEADS, DIFF_V)))
    outs_s = (stack(new_s, 0, (depth, bd, sd, kv_lora)), stack(new_s, 1, (depth, bd, sd, MLA_ROPE)),
              stack(new_s, 2, (depth, bd, sd, DIFF_HEADS, 2, DIFF_DH)),
              stack(new_s, 3, (depth, bd, sd, DIFF_HEADS, DIFF_V)))
    return (xp, xs_tok.reshape(bd, sd, d)) + outs_p + outs_s
```

```python
import functools
import math

import jax
import jax.numpy as jnp
import numpy as np
from jax import lax
from jax.experimental import pallas as pl
from jax.experimental.pallas import tpu as pltpu

F32 = jnp.float32
BF16 = jnp.bfloat16
NEG_INF = float("-inf")

CHUNK = 64
EPS = 1e-6
MLA_HEADS = 8
MLA_NOPE = 64
MLA_ROPE = 32
MLA_V = 64
ROPE_THETA = 10000.0
MLA_SCALE = (MLA_NOPE + MLA_ROPE) ** -0.5
DIFF_HEADS = 4
DIFF_DH = 64
DIFF_V = 2 * DIFF_DH
DIFF_SCALE = DIFF_DH ** -0.5
NUM_BUCKETS = 32
MAX_DISTANCE = 128
N_GROUPS = 8
TOPK_GROUPS = 4
TOP_K = 8
ROUTE_SCALE = 2.5

LANES = 128
HEAD_PAD = LANES
VMEM_LIMIT = 56 * 1024 * 1024

LOG2E = math.log2(math.e)

ROW_TILE = 256
ATTN_TILE = 512
EXPERT_BLOCK = 256


def _cparams(*sem):
    return pltpu.CompilerParams(dimension_semantics=sem, vmem_limit_bytes=VMEM_LIMIT)


def _nt(a, b):
    return lax.dot_general(a, b, (((1,), (1,)), ((), ())), preferred_element_type=F32)


def _mm(a, b):
    return jnp.dot(a, b, preferred_element_type=F32)


def _rms(x, g):
    return x * lax.rsqrt(jnp.mean(x * x, axis=-1, keepdims=True) + EPS) * g


def _pack_bf16_pairs(a):
    w = a.shape[1] // 2
    lo = lax.bitcast_convert_type(a[:, :w].astype(BF16).astype(F32), jnp.uint32)
    hi = lax.bitcast_convert_type(a[:, w:].astype(BF16).astype(F32), jnp.uint32)
    return (lo >> 16) | (hi & jnp.uint32(0xFFFF0000))


def _unpack_bf16_pairs(u):
    lo = lax.bitcast_convert_type(u << 16, F32)
    hi = lax.bitcast_convert_type(u & jnp.uint32(0xFFFF0000), F32)
    return lo, hi


def _ada_kernel(c_ref, w_ref, b_ref, o_ref):
    c = c_ref[...]
    s = (c * jax.nn.sigmoid(c)).astype(BF16)
    o_ref[...] = _mm(s, w_ref[...].astype(BF16)) + b_ref[...]


def _ada(c, w, b):
    n, d = c.shape
    cols = w.shape[1]
    tn = 1024
    return pl.pallas_call(
        _ada_kernel,
        grid=(cols // tn,),
        in_specs=[pl.BlockSpec((n, d), lambda j: (0, 0)),
                  pl.BlockSpec((d, tn), lambda j: (0, j)),
                  pl.BlockSpec((1, tn), lambda j: (0, j))],
        out_specs=pl.BlockSpec((n, tn), lambda j: (0, j)),
        out_shape=jax.ShapeDtypeStruct((n, cols), F32),
        compiler_params=_cparams("arbitrary"),
        name="ada_mod",
    )(c, w, b.reshape(1, cols))


def _pre_kernel(x_ref, sc_ref, sh_ref, g_ref, win_ref, gq_ref, gkv_ref, wq_ref, wkv_ref,
                ck_ref, sk_ref, cq_ref, sq_ref,
                lat_ref, kpe_ref, dk_ref, dv_ref, q_ref, k_ref, vt_ref, dqb_ref, dkb_ref, dvt_ref,
                *, q_lora, kv_lora, diff_qk, diff_w):
    h = _rms(x_ref[...], g_ref[...]) * (1.0 + sc_ref[...]) + sh_ref[...]
    z = _mm(h.astype(BF16), win_ref[...])
    o = 0
    c_q = z[:, o:o + q_lora]; o += q_lora
    c_kv = z[:, o:o + kv_lora]; o += kv_lora
    dq = z[:, o:o + diff_qk]; o += diff_qk
    dk = z[:, o:o + diff_qk]; o += diff_qk
    dv = z[:, o:o + diff_w]; o += diff_w
    kpa = z[:, o:o + HEAD_PAD]; o += HEAD_PAD
    kps = z[:, o:o + HEAD_PAD]
    lat = _rms(c_kv, gkv_ref[...])
    kr = kpa * ck_ref[...] + kps * sk_ref[...]
    lat_ref[...] = lat
    kpe_ref[...] = kr[:, :MLA_ROPE]
    dk_ref[...] = dk
    dv_ref[...] = dv
    qn = _rms(c_q, gq_ref[...]).astype(BF16)
    zq = _mm(qn, wq_ref[...])
    qw = MLA_HEADS * HEAD_PAD
    cq = jnp.concatenate([cq_ref[...]] * MLA_HEADS, axis=1)
    sq = jnp.concatenate([sq_ref[...]] * MLA_HEADS, axis=1)
    q_ref[...] = (zq[:, :qw] * cq + zq[:, qw:] * sq).astype(BF16)
    zk = _mm(lat.astype(BF16), wkv_ref[...])
    k_ref[...] = (zk[:, :qw] + jnp.concatenate([kr] * MLA_HEADS, axis=1)).astype(BF16)
    vt_ref[...] = zk[:, qw:].T.astype(BF16)
    dqb_ref[...] = (dq * (DIFF_SCALE * LOG2E)).astype(BF16)
    dkb_ref[...] = dk.astype(BF16)
    dvt_ref[...] = dv.T.astype(BF16)


def _pre(x, sc, sh, g, win, gq, gkv, wq, wkv, ck, sk, cq, sq, dims, tm):
    nb, n, d = x.shape
    q_lora, kv_lora, diff_qk, diff_w = dims
    mrows = sc.shape[1]
    mt = 1 if mrows == 1 else tm
    tok = lambda w: pl.BlockSpec((None, tm, w), lambda b, i: (b, i, 0))
    tr = lambda w: pl.BlockSpec((None, None, w, tm), lambda b, i: (b, i, 0, 0))
    mod = pl.BlockSpec((None, mt, d), (lambda b, i: (b, 0, 0)) if mrows == 1 else (lambda b, i: (b, i, 0)))
    full = lambda a: pl.BlockSpec(a.shape, lambda b, i: (0,) * a.ndim)
    tab = pl.BlockSpec((tm, HEAD_PAD), lambda b, i: (i, 0))
    qw = MLA_HEADS * HEAD_PAD
    vw = MLA_HEADS * MLA_V
    rows = lambda w, t: (tok(w), jax.ShapeDtypeStruct((nb, n, w), t))
    cols = lambda w, t: (tr(w), jax.ShapeDtypeStruct((nb, n // tm, w, tm), t))
    outs = [rows(kv_lora, F32), rows(MLA_ROPE, F32), rows(diff_qk, F32), rows(diff_w, F32),
            rows(qw, BF16), rows(qw, BF16), cols(vw, BF16), rows(diff_qk, BF16), rows(diff_qk, BF16),
            cols(diff_w, BF16)]
    return pl.pallas_call(
        functools.partial(_pre_kernel, q_lora=q_lora, kv_lora=kv_lora, diff_qk=diff_qk, diff_w=diff_w),
        grid=(nb, n // tm),
        in_specs=[tok(d), mod, mod, full(g), full(win), full(gq), full(gkv), full(wq), full(wkv),
                  tab, tab, tab, tab],
        out_specs=[o[0] for o in outs],
        out_shape=[o[1] for o in outs],
        compiler_params=_cparams("parallel", "arbitrary"),
        name="pre_attn",
    )(x, sc, sh, g, win, gq, gkv, wq, wkv, ck, sk, cq, sq)


def _online_update(s_t, v_t, m_ref, l_ref, acc_ref):
    m_prev = m_ref[...]
    m_new = jnp.maximum(m_prev, jnp.max(s_t, axis=0, keepdims=True))
    alpha = jnp.exp2(m_prev - m_new)
    p = jnp.exp2(s_t - m_new)
    l_ref[...] = alpha * l_ref[...] + jnp.sum(p, axis=0, keepdims=True)
    acc_ref[...] = alpha * acc_ref[...] + _mm(v_t, p.astype(BF16))
    m_ref[...] = m_new


def _init_stats(m_ref, l_ref, acc_ref):
    m_ref[...] = jnp.full(m_ref.shape, NEG_INF, F32)
    l_ref[...] = jnp.zeros(l_ref.shape, F32)
    acc_ref[...] = jnp.zeros(acc_ref.shape, F32)


def _mla_kernel(q_ref, k_ref, vt_ref, o_ref, m_ref, l_ref, acc_ref, *, tile):
    qi = pl.program_id(2)
    key = lax.broadcasted_iota(jnp.int32, (tile, tile), 0)
    qry = lax.broadcasted_iota(jnp.int32, (tile, tile), 1)
    visible = (key // CHUNK) <= (qry // CHUNK)
    _init_stats(m_ref, l_ref, acc_ref)

    def step(j, mask):
        rows = pl.ds(pl.multiple_of(j * tile, tile), tile)
        v_t = vt_ref[j]
        for hh in range(2):
            lanes = slice(hh * HEAD_PAD, (hh + 1) * HEAD_PAD)
            s_t = _nt(k_ref[rows, lanes], q_ref[:, lanes])
            if mask is not None:
                s_t = jnp.where(mask, s_t, NEG_INF)
            _online_update(s_t, v_t, m_ref.at[hh], l_ref.at[hh], acc_ref.at[hh])

    def body(j, carry):
        step(j, None)
        return carry

    lax.fori_loop(0, qi, body, 0)
    step(qi, visible)
    feat = lax.broadcasted_iota(jnp.int32, acc_ref.shape[1:], 0)
    o_t = jnp.where(feat < MLA_V, acc_ref[0] / l_ref[0], acc_ref[1] / l_ref[1])
    o_ref[...] = o_t.T.astype(BF16)


def _mla_prompt(q, k, v_t):
    b, s, _ = q.shape
    tiles, tile = v_t.shape[1], v_t.shape[3]
    pairs = MLA_HEADS // 2
    return pl.pallas_call(
        functools.partial(_mla_kernel, tile=tile),
        grid=(b, pairs, tiles),
        in_specs=[pl.BlockSpec((None, tile, 2 * HEAD_PAD), lambda bi, h, i: (bi, i, h)),
                  pl.BlockSpec((None, s, 2 * HEAD_PAD), lambda bi, h, i: (bi, 0, h)),
                  pl.BlockSpec((None, tiles, 2 * MLA_V, tile), lambda bi, h, i: (bi, 0, h, 0))],
        out_specs=pl.BlockSpec((None, tile, 2 * MLA_V), lambda bi, h, i: (bi, i, h)),
        out_shape=jax.ShapeDtypeStruct((b, s, MLA_HEADS * MLA_V), BF16),
        scratch_shapes=[pltpu.VMEM((2, 1, tile), F32), pltpu.VMEM((2, 1, tile), F32),
                        pltpu.VMEM((2, 2 * MLA_V, tile), F32)],
        compiler_params=_cparams("parallel", "parallel", "arbitrary"),
        name="mla_prompt",
    )(q, k, v_t)


def _lambda(lq1, lk1, lq2, lk2, lam_init):
    return (jnp.exp(jnp.sum(lq1[...] * lk1[...], keepdims=True))
            - jnp.exp(jnp.sum(lq2[...] * lk2[...], keepdims=True)) + lam_init)


def _diff_finish(o0, o1, lam, subln, lam_init):
    return (_rms(o0 - lam * o1, subln) * (1.0 - lam_init)).astype(BF16)


def _split_maps(q):
    lane = lax.broadcasted_iota(jnp.int32, q.shape, 1)
    zero = jnp.zeros_like(q)
    return jnp.where(lane < DIFF_DH, q, zero), jnp.where(lane >= DIFF_DH, q, zero)


def _diff_kernel(far_ref, q_ref, k_ref, vt_ref, bias_ref, lq1, lk1, lq2, lk2, subln_ref, o_ref,
                 m_ref, l_ref, acc_ref, *, tile, lam_init):
    h = pl.program_id(1)
    qi = pl.program_id(2)
    qm = _split_maps(q_ref[...])
    _init_stats(m_ref, l_ref, acc_ref)

    def step(j, which):
        rows = pl.ds(pl.multiple_of(j * tile, tile), tile)
        k = k_ref[rows, :]
        v_t = vt_ref[j]
        for m in range(2):
            s_t = _nt(k, qm[m])
            s_t = s_t + (far_ref[h, m] if which is None else bias_ref[m, which])
            _online_update(s_t, v_t, m_ref.at[m], l_ref.at[m], acc_ref.at[m])

    def body(j, carry):
        step(j, None)
        return carry

    lax.fori_loop(0, jnp.maximum(qi - 1, 0), body, 0)

    @pl.when(qi >= 1)
    def _():
        step(qi - 1, 1)

    step(qi, 0)
    lam = _lambda(lq1, lk1, lq2, lk2, lam_init)
    o0 = (acc_ref[0] / l_ref[0]).T
    o1 = (acc_ref[1] / l_ref[1]).T
    o_ref[...] = _diff_finish(o0, o1, lam, subln_ref[...], lam_init)


def _diff_prompt(far, q, k, v_t, bias, lq1, lk1, lq2, lk2, subln, lam_init):
    b, s, _ = q.shape
    tiles, tile = v_t.shape[1], v_t.shape[3]
    vec = lambda a: pl.BlockSpec(a.shape, lambda bi, h, i: (0, 0))
    return pl.pallas_call(
        functools.partial(_diff_kernel, tile=tile, lam_init=lam_init),
        grid=(b, DIFF_HEADS, tiles),
        in_specs=[pl.BlockSpec(memory_space=pltpu.SMEM),
                  pl.BlockSpec((None, tile, DIFF_V), lambda bi, h, i: (bi, i, h)),
                  pl.BlockSpec((None, s, DIFF_V), lambda bi, h, i: (bi, 0, h)),
                  pl.BlockSpec((None, tiles, DIFF_V, tile), lambda bi, h, i: (bi, 0, h, 0)),
                  pl.BlockSpec((None, 2, 2, tile, tile), lambda bi, h, i: (h, 0, 0, 0, 0)),
                  vec(lq1), vec(lk1), vec(lq2), vec(lk2), vec(subln)],
        out_specs=pl.BlockSpec((None, tile, DIFF_V), lambda bi, h, i: (bi, i, h)),
        out_shape=jax.ShapeDtypeStruct((b, s, DIFF_HEADS * DIFF_V), BF16),
        scratch_shapes=[pltpu.VMEM((2, 1, tile), F32), pltpu.VMEM((2, 1, tile), F32),
                        pltpu.VMEM((2, DIFF_V, tile), F32)],
        compiler_params=_cparams("parallel", "parallel", "arbitrary"),
        name="diff_prompt",
    )(far, q, k, v_t, bias, lq1, lk1, lq2, lk2, subln)


def _softmax2(s_c, s_n):
    m = jnp.maximum(jnp.max(s_c, axis=-1, keepdims=True), jnp.max(s_n, axis=-1, keepdims=True))
    p_c = jnp.exp2(s_c - m)
    p_n = jnp.exp2(s_n - m)
    l = jnp.sum(p_c, axis=-1, keepdims=True) + jnp.sum(p_n, axis=-1, keepdims=True)
    return p_c.astype(BF16), p_n.astype(BF16), l


def _mla_sample_kernel(q_ref, latc_ref, kpec_ref, latn_ref, kpen_ref, wk_ref, wv_ref, mask_ref, o_ref):
    q = q_ref[...]
    latc = latc_ref[...].astype(BF16)
    kpec = kpec_ref[...].astype(BF16)
    latn = latn_ref[...].astype(BF16)
    kpen = kpen_ref[...].astype(BF16)
    qlat, qrope = [], []
    for h in range(MLA_HEADS):
        lanes = slice(h * HEAD_PAD, (h + 1) * HEAD_PAD)
        qh = q[:, lanes]
        qlat.append(_nt(qh, wk_ref[:, lanes]).astype(BF16))
        qrope.append(qh[:, :MLA_ROPE])
    qlat = jnp.concatenate(qlat, axis=0)
    qrope = jnp.concatenate(qrope, axis=0)
    s_c = _nt(qlat, latc) + _nt(qrope, kpec)
    s_n = _nt(qlat, latn) + _nt(qrope, kpen) + jnp.concatenate([mask_ref[...]] * MLA_HEADS, axis=0)
    p_c, p_n, l = _softmax2(s_c, s_n)
    o_lat = ((_mm(p_c, latc) + _mm(p_n, latn)) / l).astype(BF16)
    n = q.shape[0]
    lane = lax.broadcasted_iota(jnp.int32, (n, MLA_HEADS * MLA_V), 1)
    out = jnp.zeros((n, MLA_HEADS * MLA_V), F32)
    for h in range(MLA_HEADS):
        oh = _mm(o_lat[h * n:(h + 1) * n], wv_ref[...])
        out = jnp.where(lane // MLA_V == h, oh, out)
    o_ref[...] = out.astype(BF16)


def _mla_sample(q, latc, kpec, latn, kpen, wk, wv, mask):
    bd, n, _ = q.shape
    p = latc.shape[1]
    per = lambda a: pl.BlockSpec((None,) + a.shape[1:], lambda b: (b, 0, 0))
    full = lambda a: pl.BlockSpec(a.shape, lambda b: (0, 0))
    return pl.pallas_call(
        _mla_sample_kernel,
        grid=(bd,),
        in_specs=[per(q), per(latc), per(kpec), per(latn), per(kpen), full(wk), full(wv), full(mask)],
        out_specs=pl.BlockSpec((None, n, MLA_HEADS * MLA_V), lambda b: (b, 0, 0)),
        out_shape=jax.ShapeDtypeStruct((bd, n, MLA_HEADS * MLA_V), BF16),
        compiler_params=_cparams("parallel"),
        name="mla_sample",
    )(q, latc, kpec, latn, kpen, wk, wv, mask)


def _diff_sample_kernel(q_ref, kc_ref, vc_ref, kn_ref, vn_ref, bc_ref, bn_ref, lq1, lk1, lq2, lk2, subln_ref,
                        o_ref, *, lam_init):
    qm = _split_maps(q_ref[...])
    kc = kc_ref[...].astype(BF16)
    vc = vc_ref[...].astype(BF16)
    kn = kn_ref[...].astype(BF16)
    vn = vn_ref[...].astype(BF16)
    outs = []
    for m in range(2):
        s_c = _nt(qm[m], kc) + bc_ref[m]
        s_n = _nt(qm[m], kn) + bn_ref[m]
        p_c, p_n, l = _softmax2(s_c, s_n)
        outs.append((_mm(p_c, vc) + _mm(p_n, vn)) / l)
    lam = _lambda(lq1, lk1, lq2, lk2, lam_init)
    o_ref[...] = _diff_finish(outs[0], outs[1], lam, subln_ref[...], lam_init)


def _diff_sample(q, kc, vc, kn, vn, bias_c, bias_n, lq1, lk1, lq2, lk2, subln, lam_init):
    bd, n, _ = q.shape
    p = kc.shape[1]
    vec = lambda a: pl.BlockSpec(a.shape, lambda b, h: (0, 0))
    head = lambda rows: pl.BlockSpec((None, rows, DIFF_V), lambda b, h: (b, 0, h))
    return pl.pallas_call(
        functools.partial(_diff_sample_kernel, lam_init=lam_init),
        grid=(bd, DIFF_HEADS),
        in_specs=[head(n), head(p), head(p), head(n), head(n),
                  pl.BlockSpec((None, 2, n, p), lambda b, h: (h, 0, 0, 0)),
                  pl.BlockSpec((None, 2, n, n), lambda b, h: (h, 0, 0, 0)),
                  vec(lq1), vec(lk1), vec(lq2), vec(lk2), vec(subln)],
        out_specs=head(n),
        out_shape=jax.ShapeDtypeStruct((bd, n, DIFF_HEADS * DIFF_V), BF16),
        compiler_params=_cparams("parallel", "arbitrary"),
        name="diff_sample",
    )(q, kc, vc, kn, vn, bias_c, bias_n, lq1, lk1, lq2, lk2, subln)


def _route(logits, rb):
    n_exp, n = logits.shape
    gsz = n_exp // N_GROUPS
    scores = jax.nn.sigmoid(logits)
    biased = scores + rb
    io_g = lax.broadcasted_iota(jnp.int32, (gsz, n), 0)
    gscore = []
    for g in range(N_GROUPS):
        v = biased[g * gsz:(g + 1) * gsz]
        m1 = jnp.max(v, axis=0, keepdims=True)
        i1 = jnp.min(jnp.where(v == m1, io_g, gsz), axis=0, keepdims=True)
        m2 = jnp.max(jnp.where(io_g == i1, NEG_INF, v), axis=0, keepdims=True)
        gscore.append(m1 + m2)
    gscore = jnp.concatenate(gscore, axis=0)
    io_n = lax.broadcasted_iota(jnp.int32, (N_GROUPS, n), 0)
    chosen = jnp.zeros((N_GROUPS, n), jnp.int32)
    for _ in range(TOPK_GROUPS):
        m = jnp.max(gscore, axis=0, keepdims=True)
        i = jnp.min(jnp.where(gscore == m, io_n, N_GROUPS), axis=0, keepdims=True)
        hit = io_n == i
        chosen = jnp.where(hit, 1, chosen)
        gscore = jnp.where(hit, NEG_INF, gscore)
    masked = jnp.concatenate(
        [jnp.where(chosen[g:g + 1] > 0, biased[g * gsz:(g + 1) * gsz], NEG_INF) for g in range(N_GROUPS)], axis=0)
    io_e = lax.broadcasted_iota(jnp.int32, (n_exp, n), 0)
    idx, sel = [], []
    for _ in range(TOP_K):
        m = jnp.max(masked, axis=0, keepdims=True)
        i = jnp.min(jnp.where(masked == m, io_e, n_exp), axis=0, keepdims=True)
        hit = io_e == i
        idx.append(i)
        sel.append(jnp.sum(jnp.where(hit, scores, 0.0), axis=0, keepdims=True))
        masked = jnp.where(hit, NEG_INF, masked)
    sel = jnp.concatenate(sel, axis=0)
    gates = sel / jnp.sum(sel, axis=0, keepdims=True) * ROUTE_SCALE
    picked = jnp.zeros((n_exp, n), F32)
    for i in idx:
        picked = picked + (io_e == i).astype(F32)
    picked = picked.astype(BF16)
    earlier = (lax.broadcasted_iota(jnp.int32, (n, n), 0) < lax.broadcasted_iota(jnp.int32, (n, n), 1)).astype(BF16)
    before = _mm(picked, earlier)
    rank = [jnp.sum(jnp.where(io_e == i, before, 0.0), axis=0, keepdims=True) for i in idx]
    counts = _nt(jnp.ones((8, n), BF16), picked)[0:1]
    return (jnp.concatenate(idx, axis=0), gates, jnp.concatenate(rank, axis=0).astype(jnp.int32),
            counts.astype(jnp.int32))


def _post_kernel(x_ref, om_ref, od_ref, ga_ref, scf_ref, shf_ref, gf_ref, nf_ref, wo_ref, wrh_ref, wrl_ref,
                 rb_ref, wsg_ref, wsd_ref, base_ref, h2p_ref, idx_ref, gate_ref, rank_ref, cnt_ref):
    mw = om_ref.shape[1]
    attn = _mm(om_ref[...], wo_ref[:mw, :]) + _mm(od_ref[...], wo_ref[mw:, :])
    x1 = x_ref[...] + ga_ref[...] * attn
    h2 = _rms(x1, nf_ref[...]) * (1.0 + scf_ref[...]) + shf_ref[...]
    hb = h2.astype(BF16)
    hlo = (h2 - hb.astype(F32)).astype(BF16)
    logits = _nt(wrh_ref[...], hb) + _nt(wrh_ref[...], hlo) + _nt(wrl_ref[...], hb)
    idx, gates, rank, counts = _route(logits, rb_ref[...])
    idx_ref[...] = idx
    gate_ref[...] = gates
    rank_ref[...] = rank
    cnt_ref[...] = counts
    gu = _mm(hb, wsg_ref[...])
    ff = gu.shape[1] // 2
    g, u = gu[:, :ff], gu[:, ff:]
    shared = _mm((g * jax.nn.sigmoid(g) * u).astype(BF16), wsd_ref[...])
    base_ref[...] = x1 + gf_ref[...] * shared
    h2p_ref[...] = _pack_bf16_pairs(h2)


def _post(x, om, od, ga, scf, shf, gf, nf, wo, wrh, wrl, rb, wsg, wsd):
    nb, n, d = x.shape
    tm = min(ROW_TILE, n)
    mrows = ga.shape[1]
    mt = 1 if mrows == 1 else tm
    tok = lambda w: pl.BlockSpec((None, tm, w), lambda b, i: (b, i, 0))
    mod = pl.BlockSpec((None, mt, d), (lambda b, i: (b, 0, 0)) if mrows == 1 else (lambda b, i: (b, i, 0)))
    full = lambda a: pl.BlockSpec(a.shape, lambda b, i: (0,) * a.ndim)
    kt = pl.BlockSpec((None, TOP_K, tm), lambda b, i: (b, 0, i))
    n_exp = rb.shape[0]
    return pl.pallas_call(
        _post_kernel,
        grid=(nb, n // tm),
        in_specs=[tok(d), tok(om.shape[2]), tok(od.shape[2]), mod, mod, mod, mod, full(nf), full(wo),
                  full(wrh), full(wrl), full(rb), full(wsg), full(wsd)],
        out_specs=[tok(d), tok(d // 2), kt, kt, kt,
                   pl.BlockSpec((None, None, 1, n_exp), lambda b, i: (b, i, 0, 0))],
        out_shape=[jax.ShapeDtypeStruct((nb, n, d), F32), jax.ShapeDtypeStruct((nb, n, d // 2), jnp.uint32),
                   jax.ShapeDtypeStruct((nb, TOP_K, n), jnp.int32), jax.ShapeDtypeStruct((nb, TOP_K, n), F32),
                   jax.ShapeDtypeStruct((nb, TOP_K, n), jnp.int32),
                   jax.ShapeDtypeStruct((nb, n // tm, 1, n_exp), jnp.int32)],
        compiler_params=_cparams("parallel", "arbitrary"),
        name="post_attn",
    )(x, om, od, ga, scf, shf, gf, nf, wo, wrh, wrl, rb, wsg, wsd)


def _dest_kernel(idx_ref, rank_ref, base_ref, dest_ref):
    n_exp = base_ref.shape[0]
    n = idx_ref.shape[1]
    io_e = lax.broadcasted_iota(jnp.int32, (n_exp, n), 0)
    base = base_ref[...]
    rows = [jnp.sum(jnp.where(io_e == idx_ref[k:k + 1, :], base, 0.0), axis=0, keepdims=True)
            for k in range(TOP_K)]
    dest_ref[...] = jnp.concatenate(rows, axis=0).astype(jnp.int32) + rank_ref[...]


def _dest(idx, rank, base):
    nb, _, n = idx.shape
    tiles, n_exp = base.shape[1], base.shape[2]
    tm = n // tiles
    kt = pl.BlockSpec((None, TOP_K, tm), lambda b, i: (b, 0, i))
    return pl.pallas_call(
        _dest_kernel,
        grid=(nb, tiles),
        in_specs=[kt, kt, pl.BlockSpec((None, None, n_exp, 1), lambda b, i: (b, i, 0, 0))],
        out_specs=pl.BlockSpec((None, None, TOP_K, tm), lambda b, i: (b, i, 0, 0)),
        out_shape=jax.ShapeDtypeStruct((nb, tiles, TOP_K, tm), jnp.int32),
        compiler_params=_cparams("parallel", "arbitrary"),
        name="moe_slots",
    )(idx, rank, base)


def _row_copy_wait(src_ref, dst_ref, sem, rows):
    pltpu.make_async_copy(src_ref.at[pl.ds(0, rows)], dst_ref.at[pl.ds(0, rows)], sem).wait()


def _dispatch_kernel(dest_ref, h_ref, xs_in_ref, xs_ref, sem):
    del xs_in_ref
    n = h_ref.shape[0]

    def body(r, carry):
        for k in range(TOP_K):
            pltpu.make_async_copy(h_ref.at[pl.ds(r, 1)], xs_ref.at[pl.ds(dest_ref[k, r], 1)], sem).start()
        return carry

    lax.fori_loop(0, n, body, 0)
    for _ in range(TOP_K):
        _row_copy_wait(h_ref, xs_ref, sem, n)


def _dispatch(dest, h2p, xs):
    tiles, _, rows = dest.shape
    w = h2p.shape[1]
    return pl.pallas_call(
        _dispatch_kernel,
        grid=(tiles,),
        in_specs=[pl.BlockSpec((None, TOP_K, rows), lambda i: (i, 0, 0), memory_space=pltpu.SMEM),
                  pl.BlockSpec((rows, w), lambda i: (i, 0)),
                  pl.BlockSpec(memory_space=pl.ANY)],
        out_specs=pl.BlockSpec(memory_space=pl.ANY),
        out_shape=jax.ShapeDtypeStruct(xs.shape, xs.dtype),
        scratch_shapes=[pltpu.SemaphoreType.DMA(())],
        input_output_aliases={2: 0},
        compiler_params=_cparams("arbitrary"),
        name="moe_dispatch",
    )(dest, h2p, xs)


def _expert_kernel(blk_e_ref, used_ref, xs_ref, wgu_ref, wd_ref, ys_ref, wgu_bf, wd_bf):
    i = pl.program_id(0)
    prev = blk_e_ref[jnp.maximum(i - 1, 0)]

    @pl.when((i == 0) | (blk_e_ref[i] != prev))
    def _():
        wgu_bf[...] = wgu_ref[...].astype(BF16)
        wd_bf[...] = wd_ref[...].astype(BF16)

    @pl.when(i < used_ref[0])
    def _():
        lo, hi = _unpack_bf16_pairs(xs_ref[...])
        w = lo.shape[1]
        gu = _mm(lo.astype(BF16), wgu_bf[:w, :]) + _mm(hi.astype(BF16), wgu_bf[w:, :])
        ff = gu.shape[1] // 2
        g, u = gu[:, :ff], gu[:, ff:]
        y = _mm((g * jax.nn.sigmoid(g) * u).astype(BF16), wd_bf[...])
        ys_ref[...] = _pack_bf16_pairs(y)

    @pl.when(i >= used_ref[0])
    def _():
        ys_ref[...] = jnp.zeros(ys_ref.shape, ys_ref.dtype)


def _experts(blk_e, used, xs, w_gu, w_down):
    n_slots, w = xs.shape
    bm = EXPERT_BLOCK
    _, d, ff2 = w_gu.shape
    grid_spec = pltpu.PrefetchScalarGridSpec(
        num_scalar_prefetch=2,
        grid=(n_slots // bm,),
        in_specs=[pl.BlockSpec((bm, w), lambda i, be, nu: (i, 0)),
                  pl.BlockSpec((None, d, ff2), lambda i, be, nu: (be[i], 0, 0)),
                  pl.BlockSpec((None, ff2 // 2, d), lambda i, be, nu: (be[i], 0, 0))],
        out_specs=pl.BlockSpec((bm, w), lambda i, be, nu: (i, 0)),
        scratch_shapes=[pltpu.VMEM((d, ff2), BF16), pltpu.VMEM((ff2 // 2, d), BF16)],
    )
    return pl.pallas_call(
        _expert_kernel,
        grid_spec=grid_spec,
        out_shape=jax.ShapeDtypeStruct((n_slots, w), jnp.uint32),
        compiler_params=_cparams("arbitrary"),
        name="moe_experts",
    )(blk_e, used, xs, w_gu, w_down)


def _combine_kernel(dest_ref, gate_ref, base_ref, gf_ref, fn_ref, ys_ref, o_ref, buf, sem):
    n = base_ref.shape[0]

    def body(r, carry):
        for k in range(TOP_K):
            pltpu.make_async_copy(ys_ref.at[pl.ds(dest_ref[k, r], 1)], buf.at[k, pl.ds(r, 1)], sem).start()
        return carry

    lax.fori_loop(0, n, body, 0)
    for k in range(TOP_K):
        _row_copy_wait(ys_ref, buf.at[k], sem, n)
    gates = gate_ref[...]
    lo_acc = None
    for k in range(TOP_K):
        lo, hi = _unpack_bf16_pairs(buf[k])
        g = gates[:, k:k + 1]
        lo_acc = g * lo if lo_acc is None else lo_acc + g * lo
        hi_acc = g * hi if k == 0 else hi_acc + g * hi
    routed = jnp.concatenate([lo_acc, hi_acc], axis=1)
    o_ref[...] = _rms(base_ref[...] + gf_ref[...] * routed, fn_ref[...])


def _combine(dest, gates, base, gf, fn, ys):
    nb, n, d = base.shape
    rows = dest.shape[-1]
    mrows = gf.shape[1]
    mod = pl.BlockSpec((None, 1 if mrows == 1 else rows, d),
                       (lambda b, i: (b, 0, 0)) if mrows == 1 else (lambda b, i: (b, i, 0)))
    return pl.pallas_call(
        _combine_kernel,
        grid=(nb, n // rows),
        in_specs=[pl.BlockSpec((None, None, TOP_K, rows), lambda b, i: (b, i, 0, 0), memory_space=pltpu.SMEM),
                  pl.BlockSpec((None, rows, TOP_K), lambda b, i: (b, i, 0)),
                  pl.BlockSpec((None, rows, d), lambda b, i: (b, i, 0)),
                  mod,
                  pl.BlockSpec(fn.shape, lambda b, i: (0, 0)),
                  pl.BlockSpec(memory_space=pl.ANY)],
        out_specs=pl.BlockSpec((None, rows, d), lambda b, i: (b, i, 0)),
        out_shape=jax.ShapeDtypeStruct((nb, n, d), F32),
        scratch_shapes=[pltpu.VMEM((TOP_K, rows, d // 2), jnp.uint32), pltpu.SemaphoreType.DMA(())],
        compiler_params=_cparams("arbitrary", "arbitrary"),
        name="moe_combine",
    )(dest, gates, base, gf, fn, ys)


def _dispatch_plan(tile_counts, n_assign, bm):
    n_exp = tile_counts.shape[1]
    counts = jnp.sum(tile_counts, axis=0)
    padded = (counts + bm - 1) // bm * bm
    pad_end = jnp.cumsum(padded)
    pad_start = pad_end - padded
    base = pad_start[None, :] + jnp.cumsum(tile_counts, axis=0) - tile_counts
    n_blocks = -(-n_assign // bm) + n_exp
    first_row = jnp.arange(n_blocks, dtype=jnp.int32) * bm
    blk_e = jnp.minimum(jnp.sum((pad_end[None, :] <= first_row[:, None]).astype(jnp.int32), axis=1), n_exp - 1)
    used = (pad_end[-1] // bm).astype(jnp.int32).reshape(1)
    return base.astype(F32), blk_e.astype(jnp.int32), used, n_blocks


def _rel_bucket(rel):
    nb = NUM_BUCKETS // 2
    max_exact = nb // 2
    n = jnp.abs(rel)
    nf = jnp.maximum(n, 1).astype(F32)
    large = max_exact + (jnp.log(nf / max_exact) / math.log(MAX_DISTANCE / max_exact)
                         * (nb - max_exact)).astype(jnp.int32)
    large = jnp.minimum(large, nb - 1)
    return jnp.where(rel > 0, nb, 0) + jnp.where(n < max_exact, n, large)


def _bias_table(rel_bias, q_pos, k_pos, masked, key_major=False):
    rel = jnp.asarray(k_pos, jnp.int32)[None, :] - jnp.asarray(q_pos, jnp.int32)[:, None]
    bucket = _rel_bucket(rel)[None, None]
    tab = rel_bias.astype(F32) * LOG2E
    bias = jnp.zeros(tab.shape[1:] + rel.shape, F32)
    for i in range(tab.shape[0]):
        bias = jnp.where(bucket == i, tab[i][:, :, None, None], bias)
    if masked:
        vis = (np.asarray(k_pos)[None, :] // CHUNK) <= (np.asarray(q_pos)[:, None] // CHUNK)
        bias = jnp.where(jnp.asarray(vis), bias, NEG_INF)
    return jnp.swapaxes(bias, -1, -2) if key_major else bias


def _rope_tables(pos):
    half = MLA_ROPE // 2
    inv = ROPE_THETA ** (-jnp.arange(half, dtype=F32) / half)
    ang = pos.astype(F32)[:, None] * inv
    cos, sin = jnp.cos(ang), jnp.sin(ang)
    n = pos.shape[0]
    z = lambda w: jnp.zeros((n, w), F32)
    cc = jnp.concatenate([cos, cos], axis=1)
    ss = jnp.concatenate([-sin, sin], axis=1)
    ck = jnp.concatenate([cc, z(HEAD_PAD - MLA_ROPE)], axis=1)
    sk = jnp.concatenate([ss, z(HEAD_PAD - MLA_ROPE)], axis=1)
    qs = MLA_SCALE * LOG2E
    cq = qs * jnp.concatenate([cc, jnp.ones((n, MLA_NOPE), F32), z(HEAD_PAD - MLA_ROPE - MLA_NOPE)], axis=1)
    sq = qs * sk
    return ck, sk, cq, sq


def _layer_weights(w_in, w_uq, w_ukv, q_lora, kv_lora, diff_qk, diff_w):
    d = w_in.shape[0]
    half = MLA_ROPE // 2
    swap = np.concatenate([np.arange(half, MLA_ROPE), np.arange(half)])
    o = q_lora + kv_lora
    w_kpe = w_in[:, o:o + MLA_ROPE]
    rest = w_in[:, o + MLA_ROPE:]
    zpad = jnp.zeros((d, HEAD_PAD - MLA_ROPE), w_in.dtype)
    win = jnp.concatenate([w_in[:, :o], rest, w_kpe, zpad, w_kpe[:, swap], zpad], axis=1).astype(BF16)
    r = w_uq.shape[0]
    wq3 = w_uq.reshape(r, MLA_HEADS, MLA_NOPE + MLA_ROPE)
    nope, rope = wq3[..., :MLA_NOPE], wq3[..., MLA_NOPE:]
    z3 = lambda w: jnp.zeros((r, MLA_HEADS, w), w_uq.dtype)
    wq_a = jnp.concatenate([rope, nope, z3(HEAD_PAD - MLA_ROPE - MLA_NOPE)], axis=-1)
    wq_s = jnp.concatenate([rope[..., swap], z3(HEAD_PAD - MLA_ROPE)], axis=-1)
    wq = jnp.concatenate([wq_a.reshape(r, -1), wq_s.reshape(r, -1)], axis=1).astype(BF16)
    rk = w_ukv.shape[0]
    wkv3 = w_ukv.reshape(rk, MLA_HEADS, MLA_NOPE + MLA_V)
    zk = lambda w: jnp.zeros((rk, MLA_HEADS, w), w_ukv.dtype)
    wk = jnp.concatenate([zk(MLA_ROPE), wkv3[..., :MLA_NOPE], zk(HEAD_PAD - MLA_ROPE - MLA_NOPE)], axis=-1)
    wk = wk.reshape(rk, -1).astype(BF16)
    wv = wkv3[..., MLA_NOPE:].reshape(rk, -1).astype(BF16)
    return win, wq, wk, wv


def kernel(x_prompt, x_sample, cache_mla_latent, cache_mla_krope, cache_diff_k, cache_diff_v, c_prompt, c_sample, w_ada, b_ada, norm_attn, w_in, mla_q_norm, mla_kv_norm, w_uq, w_ukv, lambda_q1, lambda_k1, lambda_q2, lambda_k2, diff_subln, rel_bias, w_out, norm_ffn, w_router, router_bias, w_exp_gu, w_exp_down, w_shared_gu, w_shared_down, final_norm):
    b, s, d = x_prompt.shape
    bd, sd, _ = x_sample.shape
    depth = w_ada.shape[0]
    past = cache_mla_latent.shape[2]
    q_lora = mla_q_norm.shape[1]
    kv_lora = mla_kv_norm.shape[1]
    diff_qk = DIFF_HEADS * 2 * DIFF_DH
    diff_w = DIFF_HEADS * DIFF_V
    dims = (q_lora, kv_lora, diff_qk, diff_w)
    n_exp = w_router.shape[2]
    ns = bd * sd
    tile = min(ATTN_TILE, s)
    assert s % tile == 0 and tile % CHUNK == 0 and tile >= MAX_DISTANCE

    pos_p = np.arange(s)
    pos_s = past + np.arange(sd)
    tabs_p = _rope_tables(jnp.asarray(pos_p, jnp.int32))
    tabs_s = tuple(jnp.tile(t, (bd, 1)) for t in _rope_tables(jnp.asarray(pos_s, jnp.int32)))
    bias_p = jnp.stack([_bias_table(rel_bias, pos_p[:tile], pos_p[:tile], True, key_major=True),
                        _bias_table(rel_bias, pos_p[:tile] + tile, pos_p[:tile], False, key_major=True)], axis=2)
    far_p = _bias_table(rel_bias, np.array([2 * tile]), np.array([0]), False)[:, :, 0, 0]
    bias_sc = _bias_table(rel_bias, pos_s, np.arange(past), True)
    bias_sn = _bias_table(rel_bias, pos_s, pos_s, True)
    mask_sn = jnp.asarray(np.where((pos_s[None, :] // CHUNK) <= (pos_s[:, None] // CHUNK), 0.0, NEG_INF), F32)
    assert bool(np.all((np.arange(past)[None, :] // CHUNK) <= (pos_s[:, None] // CHUNK)))

    xp, xs_tok = x_prompt, x_sample.reshape(1, ns, d)
    c_all = jnp.concatenate([c_prompt, c_sample], axis=0)
    row2 = lambda a: a.reshape(1, -1)
    new_p, new_s = [], []
    for l in range(depth):
        lam_init = 0.8 - 0.6 * math.exp(-0.3 * l)
        mod = _ada(c_all, w_ada[l], b_ada[l])
        mods = jnp.split(mod, 6, axis=-1)
        mod_p = [m[:b].reshape(b, 1, d) for m in mods]
        mod_s = [jnp.repeat(m[b:], sd, axis=0).reshape(1, ns, d) for m in mods]
        win, wq, wk, wv = _layer_weights(w_in[l], w_uq[l], w_ukv[l], *dims)
        wkv = jnp.concatenate([wk, wv], axis=1)
        lam_vecs = (row2(lambda_q1[l]), row2(lambda_k1[l]), row2(lambda_q2[l]), row2(lambda_k2[l]))
        subln = row2(diff_subln[l])
        shared_pre = (row2(norm_attn[l]), win, row2(mla_q_norm[l]), row2(mla_kv_norm[l]), wq, wkv)

        lat, kpe, dk, dv, q, k, v_t, dqb, dkb, dv_t = _pre(xp, mod_p[1], mod_p[0], *shared_pre, *tabs_p, dims, tile)
        new_p.append((lat, kpe, dk, dv))
        om_p = _mla_prompt(q, k, v_t)
        od_p = _diff_prompt(far_p, dqb, dkb, dv_t, bias_p, *lam_vecs, subln, lam_init)

        lat_s, kpe_s, dk_s, dv_s, q_s, _, _, dqb_s, _, _ = _pre(
            xs_tok, mod_s[1], mod_s[0], *shared_pre, *tabs_s, dims, min(ROW_TILE, ns))
        new_s.append((lat_s, kpe_s, dk_s, dv_s))
        per = lambda a: a.reshape(bd, sd, a.shape[-1])
        om_s = _mla_sample(per(q_s), cache_mla_latent[l], cache_mla_krope[l], per(lat_s), per(kpe_s), wk, wv,
                           mask_sn)
        od_s = _diff_sample(per(dqb_s), cache_diff_k[l].reshape(bd, past, diff_qk),
                            cache_diff_v[l].reshape(bd, past, diff_w), per(dk_s), per(dv_s),
                            bias_sc, bias_sn, *lam_vecs, subln, lam_init)
        om_s = om_s.reshape(1, ns, -1)
        od_s = od_s.reshape(1, ns, -1)

        wr_t = w_router[l].T
        wrh = wr_t.astype(BF16)
        wrl = (wr_t - wrh.astype(F32)).astype(BF16)
        post_w = (row2(norm_ffn[l]), w_out[l].astype(BF16), wrh, wrl, router_bias[l].reshape(n_exp, 1),
                  w_shared_gu[l].astype(BF16), w_shared_down[l].astype(BF16))
        base_p, h2p_p, idx_p, gate_p, rank_p, cnt_p = _post(
            xp, om_p, od_p, mod_p[2], mod_p[4], mod_p[3], mod_p[5], *post_w)
        base_s, h2p_s, idx_s, gate_s, rank_s, cnt_s = _post(
            xs_tok, om_s, od_s, mod_s[2], mod_s[4], mod_s[3], mod_s[5], *post_w)

        tp = b * s
        tiles_p = cnt_p.shape[0] * cnt_p.shape[1]
        tile_counts = jnp.concatenate([cnt_p.reshape(-1, n_exp), cnt_s.reshape(-1, n_exp)], axis=0)
        slot0, blk_e, used, n_blocks = _dispatch_plan(tile_counts, (tp + ns) * TOP_K, EXPERT_BLOCK)
        dest_p = _dest(idx_p, rank_p, slot0[:tiles_p].reshape(b, -1, n_exp, 1))
        dest_s = _dest(idx_s, rank_s, slot0[tiles_p:].reshape(1, -1, n_exp, 1))
        slots = jnp.zeros((n_blocks * EXPERT_BLOCK, d // 2), jnp.uint32)
        slots = _dispatch(dest_p.reshape((-1,) + dest_p.shape[2:]), h2p_p.reshape(tp, d // 2), slots)
        slots = _dispatch(dest_s.reshape((-1,) + dest_s.shape[2:]), h2p_s.reshape(ns, d // 2), slots)
        ys = _experts(blk_e, used, slots, w_exp_gu[l], w_exp_down[l])
        fn = row2(final_norm) if l == depth - 1 else jnp.ones((1, d), F32)
        assert depth == 1
        xp = _combine(dest_p, jnp.transpose(gate_p, (0, 2, 1)), base_p, mod_p[5], fn, ys)
        xs_tok = _combine(dest_s, jnp.transpose(gate_s, (0, 2, 1)), base_s, mod_s[5], fn, ys)

    stack = lambda rows, i, shape: (rows[0][i] if depth == 1 else jnp.stack([r[i] for r in rows])).reshape(shape)
    outs_p = (stack(new_p, 0, (depth, b, s, kv_lora)), stack(new_p, 1, (depth, b, s, MLA_ROPE)),
              stack(new_p, 2, (depth, b, s, DIFF_HEADS, 2, DIFF_DH)), stack(new_p, 3, (depth, b, s, DIFF_HEADS, DIFF_V)))
    outs_s = (stack(new_s, 0, (depth, bd, sd, kv_lora)), stack(new_s, 1, (depth, bd, sd, MLA_ROPE)),
              stack(new_s, 2, (depth, bd, sd, DIFF_HEADS, 2, DIFF_DH)),
              stack(new_s, 3, (depth, bd, sd, DIFF_HEADS, DIFF_V)))
    return (xp, xs_tok.reshape(bd, sd, d)) + outs_p + outs_s
```

```python
import functools
import math

import jax
import jax.numpy as jnp
import numpy as np
from jax import lax
from jax.experimental import pallas as pl
from jax.experimental.pallas import tpu as pltpu

F32 = jnp.float32
BF16 = jnp.bfloat16
NEG_INF = float("-inf")

CHUNK = 64
EPS = 1e-6
MLA_HEADS = 8
MLA_NOPE = 64
MLA_ROPE = 32
MLA_V = 64
ROPE_THETA = 10000.0
MLA_SCALE = (MLA_NOPE + MLA_ROPE) ** -0.5
DIFF_HEADS = 4
DIFF_DH = 64
DIFF_V = 2 * DIFF_DH
DIFF_SCALE = DIFF_DH ** -0.5
NUM_BUCKETS = 32
MAX_DISTANCE = 128
N_GROUPS = 8
TOPK_GROUPS = 4
TOP_K = 8
ROUTE_SCALE = 2.5

LANES = 128
SUBLANES = 8
HEAD_PAD = LANES
VMEM_LIMIT = 56 * 1024 * 1024

LOG2E = math.log2(math.e)

ROW_TILE = 256
ATTN_TILE = 512
EXPERT_BLOCK = 256


def _cparams(*sem):
    return pltpu.CompilerParams(dimension_semantics=sem, vmem_limit_bytes=VMEM_LIMIT)


def _nt(a, b):
    return lax.dot_general(a, b, (((1,), (1,)), ((), ())), preferred_element_type=F32)


def _mm(a, b):
    return jnp.dot(a, b, preferred_element_type=F32)


def _rms(x, g):
    return x * lax.rsqrt(jnp.mean(x * x, axis=-1, keepdims=True) + EPS) * g


def _pack_bf16_pairs(a):
    w = a.shape[1] // 2
    lo = lax.bitcast_convert_type(a[:, :w].astype(BF16).astype(F32), jnp.uint32)
    hi = lax.bitcast_convert_type(a[:, w:].astype(BF16).astype(F32), jnp.uint32)
    return (lo >> 16) | (hi & jnp.uint32(0xFFFF0000))


def _unpack_bf16_pairs(u):
    lo = lax.bitcast_convert_type(u << 16, F32)
    hi = lax.bitcast_convert_type(u & jnp.uint32(0xFFFF0000), F32)
    return lo, hi


def _store_chunked(ref, u):
    n, w = u.shape
    chunks = w // LANES
    for c in range(chunks):
        ref[pl.ds(c, n, stride=chunks), :] = u[:, c * LANES:(c + 1) * LANES]


def _load_chunked(ref, n):
    chunks = ref.shape[0] // n
    return jnp.concatenate([ref[pl.ds(c, n, stride=chunks), :] for c in range(chunks)], axis=1)


def _ada_kernel(c_ref, w_ref, b_ref, o_ref):
    c = c_ref[...]
    s = (c * jax.nn.sigmoid(c)).astype(BF16)
    o_ref[...] = _mm(s, w_ref[...].astype(BF16)) + b_ref[...]


def _ada(c, w, b):
    n, d = c.shape
    cols = w.shape[1]
    tn = 1024
    return pl.pallas_call(
        _ada_kernel,
        grid=(cols // tn,),
        in_specs=[pl.BlockSpec((n, d), lambda j: (0, 0)),
                  pl.BlockSpec((d, tn), lambda j: (0, j)),
                  pl.BlockSpec((1, tn), lambda j: (0, j))],
        out_specs=pl.BlockSpec((n, tn), lambda j: (0, j)),
        out_shape=jax.ShapeDtypeStruct((n, cols), F32),
        compiler_params=_cparams("arbitrary"),
        name="ada_mod",
    )(c, w, b.reshape(1, cols))


def _pre_kernel(x_ref, sc_ref, sh_ref, g_ref, win_ref, gq_ref, gkv_ref, wq_ref, wkv_ref,
                ck_ref, sk_ref, cq_ref, sq_ref,
                lat_ref, kpe_ref, dk_ref, dv_ref, q_ref, k_ref, vt_ref, dqb_ref, dkb_ref, dvt_ref,
                *, q_lora, kv_lora, diff_qk, diff_w):
    h = _rms(x_ref[...], g_ref[...]) * (1.0 + sc_ref[...]) + sh_ref[...]
    z = _mm(h.astype(BF16), win_ref[...])
    o = 0
    c_q = z[:, o:o + q_lora]; o += q_lora
    c_kv = z[:, o:o + kv_lora]; o += kv_lora
    dq = z[:, o:o + diff_qk]; o += diff_qk
    dk = z[:, o:o + diff_qk]; o += diff_qk
    dv = z[:, o:o + diff_w]; o += diff_w
    kpa = z[:, o:o + HEAD_PAD]; o += HEAD_PAD
    kps = z[:, o:o + HEAD_PAD]
    lat = _rms(c_kv, gkv_ref[...])
    kr = kpa * ck_ref[...] + kps * sk_ref[...]
    lat_ref[...] = lat
    kpe_ref[...] = kr[:, :MLA_ROPE]
    dk_ref[...] = dk
    dv_ref[...] = dv
    qn = _rms(c_q, gq_ref[...]).astype(BF16)
    zq = _mm(qn, wq_ref[...])
    qw = MLA_HEADS * HEAD_PAD
    cq = jnp.concatenate([cq_ref[...]] * MLA_HEADS, axis=1)
    sq = jnp.concatenate([sq_ref[...]] * MLA_HEADS, axis=1)
    q_ref[...] = (zq[:, :qw] * cq + zq[:, qw:] * sq).astype(BF16)
    zk = _mm(lat.astype(BF16), wkv_ref[...])
    k_ref[...] = (zk[:, :qw] + jnp.concatenate([kr] * MLA_HEADS, axis=1)).astype(BF16)
    vt_ref[...] = zk[:, qw:].T.astype(BF16)
    dqb_ref[...] = (dq * (DIFF_SCALE * LOG2E)).astype(BF16)
    dkb_ref[...] = dk.astype(BF16)
    dvt_ref[...] = dv.T.astype(BF16)


def _pre(x, sc, sh, g, win, gq, gkv, wq, wkv, ck, sk, cq, sq, dims, tm):
    nb, n, d = x.shape
    q_lora, kv_lora, diff_qk, diff_w = dims
    mrows = sc.shape[1]
    mt = 1 if mrows == 1 else tm
    tok = lambda w: pl.BlockSpec((None, tm, w), lambda b, i: (b, i, 0))
    tr = lambda w: pl.BlockSpec((None, None, w, tm), lambda b, i: (b, i, 0, 0))
    mod = pl.BlockSpec((None, mt, d), (lambda b, i: (b, 0, 0)) if mrows == 1 else (lambda b, i: (b, i, 0)))
    full = lambda a: pl.BlockSpec(a.shape, lambda b, i: (0,) * a.ndim)
    tab = pl.BlockSpec((tm, HEAD_PAD), lambda b, i: (i, 0))
    qw = MLA_HEADS * HEAD_PAD
    vw = MLA_HEADS * MLA_V
    rows = lambda w, t: (tok(w), jax.ShapeDtypeStruct((nb, n, w), t))
    cols = lambda w, t: (tr(w), jax.ShapeDtypeStruct((nb, n // tm, w, tm), t))
    outs = [rows(kv_lora, F32), rows(MLA_ROPE, F32), rows(diff_qk, F32), rows(diff_w, F32),
            rows(qw, BF16), rows(qw, BF16), cols(vw, BF16), rows(diff_qk, BF16), rows(diff_qk, BF16),
            cols(diff_w, BF16)]
    return pl.pallas_call(
        functools.partial(_pre_kernel, q_lora=q_lora, kv_lora=kv_lora, diff_qk=diff_qk, diff_w=diff_w),
        grid=(nb, n // tm),
        in_specs=[tok(d), mod, mod, full(g), full(win), full(gq), full(gkv), full(wq), full(wkv),
                  tab, tab, tab, tab],
        out_specs=[o[0] for o in outs],
        out_shape=[o[1] for o in outs],
        compiler_params=_cparams("parallel", "arbitrary"),
        name="pre_attn",
    )(x, sc, sh, g, win, gq, gkv, wq, wkv, ck, sk, cq, sq)


def _online_update(s_t, v_t, m_ref, l_ref, acc_ref):
    m_prev = m_ref[...]
    m_new = jnp.maximum(m_prev, jnp.max(s_t, axis=0, keepdims=True))
    alpha = jnp.exp2(m_prev - m_new)
    p = jnp.exp2(s_t - m_new)
    l_ref[...] = alpha * l_ref[...] + jnp.sum(p, axis=0, keepdims=True)
    acc_ref[...] = alpha * acc_ref[...] + _mm(v_t, p.astype(BF16))
    m_ref[...] = m_new


def _init_stats(m_ref, l_ref, acc_ref):
    m_ref[...] = jnp.full(m_ref.shape, NEG_INF, F32)
    l_ref[...] = jnp.zeros(l_ref.shape, F32)
    acc_ref[...] = jnp.zeros(acc_ref.shape, F32)


def _pipelined_tiles(last, scores, consume, bufs):
    a, b = bufs
    pairs = jnp.maximum(last - 1, 0) // 2
    scores(0, a)

    def body(i, carry):
        t = 2 * i
        scores(t + 1, b)
        consume(t, a, "far")
        scores(t + 2, a)
        consume(t + 1, b, "far")
        return carry

    lax.fori_loop(0, pairs, body, 0)
    t0 = 2 * pairs
    left = last - t0

    @pl.when(left == 0)
    def _():
        consume(last, a, "diag")

    @pl.when(left == 1)
    def _():
        scores(last, b)
        consume(t0, a, "sub")
        consume(last, b, "diag")

    @pl.when(left == 2)
    def _():
        scores(t0 + 1, b)
        consume(t0, a, "far")
        scores(last, a)
        consume(t0 + 1, b, "sub")
        consume(last, a, "diag")


def _mla_kernel(q_ref, k_ref, vt_ref, o_ref, m_ref, l_ref, acc_ref, sa_ref, sb_ref, *, tile):
    qi = pl.program_id(2)
    key = lax.broadcasted_iota(jnp.int32, (tile, tile), 0)
    qry = lax.broadcasted_iota(jnp.int32, (tile, tile), 1)
    visible = (key // CHUNK) <= (qry // CHUNK)
    _init_stats(m_ref, l_ref, acc_ref)

    def scores(j, s_ref):
        rows = pl.ds(pl.multiple_of(j * tile, tile), tile)
        for hh in range(2):
            lanes = slice(hh * HEAD_PAD, (hh + 1) * HEAD_PAD)
            s_ref[hh] = _nt(k_ref[rows, lanes], q_ref[:, lanes])

    def consume(j, s_ref, kind):
        v_t = vt_ref[j]
        for hh in range(2):
            s_t = jnp.where(visible, s_ref[hh], NEG_INF) if kind == "diag" else s_ref[hh]
            _online_update(s_t, v_t, m_ref.at[hh], l_ref.at[hh], acc_ref.at[hh])

    _pipelined_tiles(qi, scores, consume, (sa_ref, sb_ref))
    feat = lax.broadcasted_iota(jnp.int32, acc_ref.shape[1:], 0)
    o_t = jnp.where(feat < MLA_V, acc_ref[0] / l_ref[0], acc_ref[1] / l_ref[1])
    o_ref[...] = o_t.T.astype(BF16)


def _mla_prompt(q, k, v_t):
    b, s, _ = q.shape
    tiles, tile = v_t.shape[1], v_t.shape[3]
    pairs = MLA_HEADS // 2
    return pl.pallas_call(
        functools.partial(_mla_kernel, tile=tile),
        grid=(b, pairs, tiles),
        in_specs=[pl.BlockSpec((None, tile, 2 * HEAD_PAD), lambda bi, h, i: (bi, i, h)),
                  pl.BlockSpec((None, s, 2 * HEAD_PAD), lambda bi, h, i: (bi, 0, h)),
                  pl.BlockSpec((None, tiles, 2 * MLA_V, tile), lambda bi, h, i: (bi, 0, h, 0))],
        out_specs=pl.BlockSpec((None, tile, 2 * MLA_V), lambda bi, h, i: (bi, i, h)),
        out_shape=jax.ShapeDtypeStruct((b, s, MLA_HEADS * MLA_V), BF16),
        scratch_shapes=[pltpu.VMEM((2, 1, tile), F32), pltpu.VMEM((2, 1, tile), F32),
                        pltpu.VMEM((2, 2 * MLA_V, tile), F32),
                        pltpu.VMEM((2, tile, tile), F32), pltpu.VMEM((2, tile, tile), F32)],
        compiler_params=_cparams("parallel", "parallel", "arbitrary"),
        name="mla_prompt",
    )(q, k, v_t)


def _lambda(lq1, lk1, lq2, lk2, lam_init):
    return (jnp.exp(jnp.sum(lq1[...] * lk1[...], keepdims=True))
            - jnp.exp(jnp.sum(lq2[...] * lk2[...], keepdims=True)) + lam_init)


def _diff_finish(o0, o1, lam, subln, lam_init):
    return (_rms(o0 - lam * o1, subln) * (1.0 - lam_init)).astype(BF16)


def _split_maps(q):
    lane = lax.broadcasted_iota(jnp.int32, q.shape, 1)
    zero = jnp.zeros_like(q)
    return jnp.where(lane < DIFF_DH, q, zero), jnp.where(lane >= DIFF_DH, q, zero)


def _diff_kernel(far_ref, q_ref, k_ref, vt_ref, bias_ref, lq1, lk1, lq2, lk2, subln_ref, o_ref,
                 m_ref, l_ref, acc_ref, sa_ref, sb_ref, *, tile, lam_init):
    h = pl.program_id(1)
    qi = pl.program_id(2)
    _init_stats(m_ref, l_ref, acc_ref)

    def scores(j, s_ref):
        k = k_ref[pl.ds(pl.multiple_of(j * tile, tile), tile), :]
        qm = _split_maps(q_ref[...])
        for m in range(2):
            s_ref[m] = _nt(k, qm[m])

    def consume(j, s_ref, kind):
        v_t = vt_ref[j]
        for m in range(2):
            bias = far_ref[h, m] if kind == "far" else bias_ref[m, 0 if kind == "diag" else 1]
            _online_update(s_ref[m] + bias, v_t, m_ref.at[m], l_ref.at[m], acc_ref.at[m])

    _pipelined_tiles(qi, scores, consume, (sa_ref, sb_ref))
    lam = _lambda(lq1, lk1, lq2, lk2, lam_init)
    o0 = (acc_ref[0] / l_ref[0]).T
    o1 = (acc_ref[1] / l_ref[1]).T
    o_ref[...] = _diff_finish(o0, o1, lam, subln_ref[...], lam_init)


def _diff_prompt(far, q, k, v_t, bias, lq1, lk1, lq2, lk2, subln, lam_init):
    b, s, _ = q.shape
    tiles, tile = v_t.shape[1], v_t.shape[3]
    vec = lambda a: pl.BlockSpec(a.shape, lambda bi, h, i: (0, 0))
    return pl.pallas_call(
        functools.partial(_diff_kernel, tile=tile, lam_init=lam_init),
        grid=(b, DIFF_HEADS, tiles),
        in_specs=[pl.BlockSpec(memory_space=pltpu.SMEM),
                  pl.BlockSpec((None, tile, DIFF_V), lambda bi, h, i: (bi, i, h)),
                  pl.BlockSpec((None, s, DIFF_V), lambda bi, h, i: (bi, 0, h)),
                  pl.BlockSpec((None, tiles, DIFF_V, tile), lambda bi, h, i: (bi, 0, h, 0)),
                  pl.BlockSpec((None, 2, 2, tile, tile), lambda bi, h, i: (h, 0, 0, 0, 0)),
                  vec(lq1), vec(lk1), vec(lq2), vec(lk2), vec(subln)],
        out_specs=pl.BlockSpec((None, tile, DIFF_V), lambda bi, h, i: (bi, i, h)),
        out_shape=jax.ShapeDtypeStruct((b, s, DIFF_HEADS * DIFF_V), BF16),
        scratch_shapes=[pltpu.VMEM((2, 1, tile), F32), pltpu.VMEM((2, 1, tile), F32),
                        pltpu.VMEM((2, DIFF_V, tile), F32),
                        pltpu.VMEM((2, tile, tile), F32), pltpu.VMEM((2, tile, tile), F32)],
        compiler_params=_cparams("parallel", "parallel", "arbitrary"),
        name="diff_prompt",
    )(far, q, k, v_t, bias, lq1, lk1, lq2, lk2, subln)


def _softmax2(s_c, s_n):
    m = jnp.maximum(jnp.max(s_c, axis=-1, keepdims=True), jnp.max(s_n, axis=-1, keepdims=True))
    p_c = jnp.exp2(s_c - m)
    p_n = jnp.exp2(s_n - m)
    l = jnp.sum(p_c, axis=-1, keepdims=True) + jnp.sum(p_n, axis=-1, keepdims=True)
    return p_c.astype(BF16), p_n.astype(BF16), l


def _mla_sample_kernel(q_ref, latc_ref, kpec_ref, latn_ref, kpen_ref, wk_ref, wv_ref, mask_ref, o_ref):
    q = q_ref[...]
    latc = latc_ref[...].astype(BF16)
    kpec = kpec_ref[...].astype(BF16)
    latn = latn_ref[...].astype(BF16)
    kpen = kpen_ref[...].astype(BF16)
    qlat, qrope = [], []
    for h in range(MLA_HEADS):
        lanes = slice(h * HEAD_PAD, (h + 1) * HEAD_PAD)
        qh = q[:, lanes]
        qlat.append(_nt(qh, wk_ref[:, lanes]).astype(BF16))
        qrope.append(qh[:, :MLA_ROPE])
    qlat = jnp.concatenate(qlat, axis=0)
    qrope = jnp.concatenate(qrope, axis=0)
    s_c = _nt(qlat, latc) + _nt(qrope, kpec)
    s_n = _nt(qlat, latn) + _nt(qrope, kpen) + jnp.concatenate([mask_ref[...]] * MLA_HEADS, axis=0)
    p_c, p_n, l = _softmax2(s_c, s_n)
    o_lat = ((_mm(p_c, latc) + _mm(p_n, latn)) / l).astype(BF16)
    n = q.shape[0]
    lane = lax.broadcasted_iota(jnp.int32, (n, MLA_HEADS * MLA_V), 1)
    out = jnp.zeros((n, MLA_HEADS * MLA_V), F32)
    for h in range(MLA_HEADS):
        oh = _mm(o_lat[h * n:(h + 1) * n], wv_ref[...])
        out = jnp.where(lane // MLA_V == h, oh, out)
    o_ref[...] = out.astype(BF16)


def _mla_sample(q, latc, kpec, latn, kpen, wk, wv, mask):
    bd, n, _ = q.shape
    p = latc.shape[1]
    per = lambda a: pl.BlockSpec((None,) + a.shape[1:], lambda b: (b, 0, 0))
    full = lambda a: pl.BlockSpec(a.shape, lambda b: (0, 0))
    return pl.pallas_call(
        _mla_sample_kernel,
        grid=(bd,),
        in_specs=[per(q), per(latc), per(kpec), per(latn), per(kpen), full(wk), full(wv), full(mask)],
        out_specs=pl.BlockSpec((None, n, MLA_HEADS * MLA_V), lambda b: (b, 0, 0)),
        out_shape=jax.ShapeDtypeStruct((bd, n, MLA_HEADS * MLA_V), BF16),
        compiler_params=_cparams("parallel"),
        name="mla_sample",
    )(q, latc, kpec, latn, kpen, wk, wv, mask)


def _diff_sample_kernel(q_ref, kc_ref, vc_ref, kn_ref, vn_ref, bc_ref, bn_ref, lq1, lk1, lq2, lk2, subln_ref,
                        o_ref, *, lam_init):
    qm = _split_maps(q_ref[...])
    kc = kc_ref[...].astype(BF16)
    vc = vc_ref[...].astype(BF16)
    kn = kn_ref[...].astype(BF16)
    vn = vn_ref[...].astype(BF16)
    outs = []
    for m in range(2):
        s_c = _nt(qm[m], kc) + bc_ref[m]
        s_n = _nt(qm[m], kn) + bn_ref[m]
        p_c, p_n, l = _softmax2(s_c, s_n)
        outs.append((_mm(p_c, vc) + _mm(p_n, vn)) / l)
    lam = _lambda(lq1, lk1, lq2, lk2, lam_init)
    o_ref[...] = _diff_finish(outs[0], outs[1], lam, subln_ref[...], lam_init)


def _diff_sample(q, kc, vc, kn, vn, bias_c, bias_n, lq1, lk1, lq2, lk2, subln, lam_init):
    bd, n, _ = q.shape
    p = kc.shape[1]
    vec = lambda a: pl.BlockSpec(a.shape, lambda b, h: (0, 0))
    head = lambda rows: pl.BlockSpec((None, rows, DIFF_V), lambda b, h: (b, 0, h))
    return pl.pallas_call(
        functools.partial(_diff_sample_kernel, lam_init=lam_init),
        grid=(bd, DIFF_HEADS),
        in_specs=[head(n), head(p), head(p), head(n), head(n),
                  pl.BlockSpec((None, 2, n, p), lambda b, h: (h, 0, 0, 0)),
                  pl.BlockSpec((None, 2, n, n), lambda b, h: (h, 0, 0, 0)),
                  vec(lq1), vec(lk1), vec(lq2), vec(lk2), vec(subln)],
        out_specs=head(n),
        out_shape=jax.ShapeDtypeStruct((bd, n, DIFF_HEADS * DIFF_V), BF16),
        compiler_params=_cparams("parallel", "arbitrary"),
        name="diff_sample",
    )(q, kc, vc, kn, vn, bias_c, bias_n, lq1, lk1, lq2, lk2, subln)


def _route(logits, rb):
    n_exp, n = logits.shape
    gsz = n_exp // N_GROUPS
    scores = jax.nn.sigmoid(logits)
    biased = scores + rb
    io_g = lax.broadcasted_iota(jnp.int32, (gsz, n), 0)
    gscore = []
    for g in range(N_GROUPS):
        v = biased[g * gsz:(g + 1) * gsz]
        m1 = jnp.max(v, axis=0, keepdims=True)
        i1 = jnp.min(jnp.where(v == m1, io_g, gsz), axis=0, keepdims=True)
        m2 = jnp.max(jnp.where(io_g == i1, NEG_INF, v), axis=0, keepdims=True)
        gscore.append(m1 + m2)
    gscore = jnp.concatenate(gscore, axis=0)
    io_n = lax.broadcasted_iota(jnp.int32, (N_GROUPS, n), 0)
    chosen = jnp.zeros((N_GROUPS, n), jnp.int32)
    for _ in range(TOPK_GROUPS):
        m = jnp.max(gscore, axis=0, keepdims=True)
        i = jnp.min(jnp.where(gscore == m, io_n, N_GROUPS), axis=0, keepdims=True)
        hit = io_n == i
        chosen = jnp.where(hit, 1, chosen)
        gscore = jnp.where(hit, NEG_INF, gscore)
    masked = jnp.concatenate(
        [jnp.where(chosen[g:g + 1] > 0, biased[g * gsz:(g + 1) * gsz], NEG_INF) for g in range(N_GROUPS)], axis=0)
    io_e = lax.broadcasted_iota(jnp.int32, (n_exp, n), 0)
    idx, sel = [], []
    for _ in range(TOP_K):
        m = jnp.max(masked, axis=0, keepdims=True)
        i = jnp.min(jnp.where(masked == m, io_e, n_exp), axis=0, keepdims=True)
        hit = io_e == i
        idx.append(i)
        sel.append(jnp.sum(jnp.where(hit, scores, 0.0), axis=0, keepdims=True))
        masked = jnp.where(hit, NEG_INF, masked)
    sel = jnp.concatenate(sel, axis=0)
    gates = sel / jnp.sum(sel, axis=0, keepdims=True) * ROUTE_SCALE
    picked = jnp.zeros((n_exp, n), F32)
    for i in idx:
        picked = picked + (io_e == i).astype(F32)
    picked = picked.astype(BF16)
    earlier = (lax.broadcasted_iota(jnp.int32, (n, n), 0) < lax.broadcasted_iota(jnp.int32, (n, n), 1)).astype(BF16)
    before = _mm(picked, earlier)
    rank = [jnp.sum(jnp.where(io_e == i, before, 0.0), axis=0, keepdims=True) for i in idx]
    counts = _nt(jnp.ones((8, n), BF16), picked)[0:1]
    return (jnp.concatenate(idx, axis=0), gates, jnp.concatenate(rank, axis=0).astype(jnp.int32),
            counts.astype(jnp.int32))


def _post_kernel(x_ref, om_ref, od_ref, ga_ref, scf_ref, shf_ref, gf_ref, nf_ref, wo_ref, wrh_ref, wrl_ref,
                 rb_ref, wsg_ref, wsd_ref, base_ref, h2p_ref, idx_ref, gate_ref, rank_ref, cnt_ref):
    mw = om_ref.shape[1]
    attn = _mm(om_ref[...], wo_ref[:mw, :]) + _mm(od_ref[...], wo_ref[mw:, :])
    x1 = x_ref[...] + ga_ref[...] * attn
    h2 = _rms(x1, nf_ref[...]) * (1.0 + scf_ref[...]) + shf_ref[...]
    hb = h2.astype(BF16)
    hlo = (h2 - hb.astype(F32)).astype(BF16)
    logits = _nt(wrh_ref[...], hb) + _nt(wrh_ref[...], hlo) + _nt(wrl_ref[...], hb)
    idx, gates, rank, counts = _route(logits, rb_ref[...])
    idx_ref[...] = idx
    gate_ref[...] = gates
    rank_ref[...] = rank
    cnt_ref[...] = counts
    gu = _mm(hb, wsg_ref[...])
    ff = gu.shape[1] // 2
    g, u = gu[:, :ff], gu[:, ff:]
    shared = _mm((g * jax.nn.sigmoid(g) * u).astype(BF16), wsd_ref[...])
    base_ref[...] = x1 + gf_ref[...] * shared
    _store_chunked(h2p_ref, _pack_bf16_pairs(h2))


def _post(x, om, od, ga, scf, shf, gf, nf, wo, wrh, wrl, rb, wsg, wsd):
    nb, n, d = x.shape
    tm = min(ROW_TILE, n)
    mrows = ga.shape[1]
    mt = 1 if mrows == 1 else tm
    tok = lambda w: pl.BlockSpec((None, tm, w), lambda b, i: (b, i, 0))
    mod = pl.BlockSpec((None, mt, d), (lambda b, i: (b, 0, 0)) if mrows == 1 else (lambda b, i: (b, i, 0)))
    full = lambda a: pl.BlockSpec(a.shape, lambda b, i: (0,) * a.ndim)
    kt = pl.BlockSpec((None, TOP_K, tm), lambda b, i: (b, 0, i))
    n_exp = rb.shape[0]
    chunks = d // 2 // LANES
    return pl.pallas_call(
        _post_kernel,
        grid=(nb, n // tm),
        in_specs=[tok(d), tok(om.shape[2]), tok(od.shape[2]), mod, mod, mod, mod, full(nf), full(wo),
                  full(wrh), full(wrl), full(rb), full(wsg), full(wsd)],
        out_specs=[tok(d), pl.BlockSpec((None, tm * chunks, LANES), lambda b, i: (b, i, 0)), kt, kt, kt,
                   pl.BlockSpec((None, None, 1, n_exp), lambda b, i: (b, i, 0, 0))],
        out_shape=[jax.ShapeDtypeStruct((nb, n, d), F32), jax.ShapeDtypeStruct((nb, n * chunks, LANES), jnp.uint32),
                   jax.ShapeDtypeStruct((nb, TOP_K, n), jnp.int32), jax.ShapeDtypeStruct((nb, TOP_K, n), F32),
                   jax.ShapeDtypeStruct((nb, TOP_K, n), jnp.int32),
                   jax.ShapeDtypeStruct((nb, n // tm, 1, n_exp), jnp.int32)],
        compiler_params=_cparams("parallel", "arbitrary"),
        name="post_attn",
    )(x, om, od, ga, scf, shf, gf, nf, wo, wrh, wrl, rb, wsg, wsd)


def _dest_kernel(idx_ref, rank_ref, base_ref, dest_ref):
    n_exp = base_ref.shape[0]
    n = idx_ref.shape[1]
    io_e = lax.broadcasted_iota(jnp.int32, (n_exp, n), 0)
    base = base_ref[...]
    rows = [jnp.sum(jnp.where(io_e == idx_ref[k:k + 1, :], base, 0.0), axis=0, keepdims=True)
            for k in range(TOP_K)]
    dest_ref[...] = jnp.concatenate(rows, axis=0).astype(jnp.int32) + rank_ref[...]


def _dest(idx, rank, base):
    nb, _, n = idx.shape
    tiles, n_exp = base.shape[1], base.shape[2]
    tm = n // tiles
    kt = pl.BlockSpec((None, TOP_K, tm), lambda b, i: (b, 0, i))
    return pl.pallas_call(
        _dest_kernel,
        grid=(nb, tiles),
        in_specs=[kt, kt, pl.BlockSpec((None, None, n_exp, 1), lambda b, i: (b, i, 0, 0))],
        out_specs=pl.BlockSpec((None, None, TOP_K, tm), lambda b, i: (b, i, 0, 0)),
        out_shape=jax.ShapeDtypeStruct((nb, tiles, TOP_K, tm), jnp.int32),
        compiler_params=_cparams("parallel", "arbitrary"),
        name="moe_slots",
    )(idx, rank, base)


def _row_copy_wait(src_ref, dst_ref, sem, rows):
    pltpu.make_async_copy(src_ref.at[pl.ds(0, rows)], dst_ref.at[pl.ds(0, rows)], sem).wait()


def _token_rows(ref, token, chunks):
    return ref.at[pl.ds(pl.multiple_of(token * chunks, chunks), chunks)]


def _dispatch_kernel(dest_ref, h_ref, xs_in_ref, xs_ref, sem, *, chunks):
    del xs_in_ref
    n = h_ref.shape[0] // chunks

    def body(g, carry):
        r0 = pl.multiple_of(g * SUBLANES, SUBLANES)
        for j in range(SUBLANES):
            for k in range(TOP_K):
                pltpu.make_async_copy(_token_rows(h_ref, r0 + j, chunks),
                                      _token_rows(xs_ref, dest_ref[k, r0 + j], chunks), sem).start()
        return carry

    lax.fori_loop(0, n // SUBLANES, body, 0)
    for _ in range(TOP_K):
        _row_copy_wait(h_ref, xs_ref, sem, n * chunks)


def _dispatch(dest, h2p, xs):
    tiles, _, rows = dest.shape
    chunks = h2p.shape[0] // (tiles * rows)
    return pl.pallas_call(
        functools.partial(_dispatch_kernel, chunks=chunks),
        grid=(tiles,),
        in_specs=[pl.BlockSpec((None, TOP_K, rows), lambda i: (i, 0, 0), memory_space=pltpu.SMEM),
                  pl.BlockSpec((rows * chunks, LANES), lambda i: (i, 0)),
                  pl.BlockSpec(memory_space=pl.ANY)],
        out_specs=pl.BlockSpec(memory_space=pl.ANY),
        out_shape=jax.ShapeDtypeStruct(xs.shape, xs.dtype),
        scratch_shapes=[pltpu.SemaphoreType.DMA(())],
        input_output_aliases={2: 0},
        compiler_params=_cparams("arbitrary"),
        name="moe_dispatch",
    )(dest, h2p, xs)


def _expert_kernel(blk_e_ref, used_ref, first_ref, next_ref, xs_ref, wgu_hbm, wd_hbm, ys_ref,
                   wgu_f32, wd_f32, wgu_bf, wd_bf, sem, *, bm):
    i = pl.program_id(0)

    def weight_copies(e, s):
        return (pltpu.make_async_copy(wgu_hbm.at[e], wgu_f32.at[s], sem.at[s, 0]),
                pltpu.make_async_copy(wd_hbm.at[e], wd_f32.at[s], sem.at[s, 1]))

    active = i < used_ref[0]

    @pl.when(active & (first_ref[i] > 0))
    def _():
        s = first_ref[i] - 1
        mine = weight_copies(blk_e_ref[i], s)

        @pl.when(i == 0)
        def _():
            for c in mine:
                c.start()

        for c in mine:
            c.wait()
        wgu_bf[...] = wgu_f32[s].astype(BF16)
        wd_bf[...] = wd_f32[s].astype(BF16)

        @pl.when(next_ref[i] >= 0)
        def _():
            for c in weight_copies(next_ref[i], 1 - s):
                c.start()

    @pl.when(active)
    def _():
        lo, hi = _unpack_bf16_pairs(_load_chunked(xs_ref, bm))
        w = lo.shape[1]
        gu = _mm(lo.astype(BF16), wgu_bf[:w, :]) + _mm(hi.astype(BF16), wgu_bf[w:, :])
        ff = gu.shape[1] // 2
        g, u = gu[:, :ff], gu[:, ff:]
        y = _mm((g * jax.nn.sigmoid(g) * u).astype(BF16), wd_bf[...])
        _store_chunked(ys_ref, _pack_bf16_pairs(y))

    @pl.when(jnp.logical_not(active))
    def _():
        ys_ref[...] = jnp.zeros(ys_ref.shape, ys_ref.dtype)


def _experts(blk_e, used, first, nxt, xs, w_gu, w_down):
    bm = EXPERT_BLOCK
    n_blocks = blk_e.shape[0]
    chunks = xs.shape[0] // (n_blocks * bm)
    _, d, ff2 = w_gu.shape
    rows = pl.BlockSpec((bm * chunks, LANES), lambda i, *_: (i, 0))
    grid_spec = pltpu.PrefetchScalarGridSpec(
        num_scalar_prefetch=4,
        grid=(n_blocks,),
        in_specs=[rows, pl.BlockSpec(memory_space=pl.ANY), pl.BlockSpec(memory_space=pl.ANY)],
        out_specs=rows,
        scratch_shapes=[pltpu.VMEM((2, d, ff2), F32), pltpu.VMEM((2, ff2 // 2, d), F32),
                        pltpu.VMEM((d, ff2), BF16), pltpu.VMEM((ff2 // 2, d), BF16),
                        pltpu.SemaphoreType.DMA((2, 2))],
    )
    return pl.pallas_call(
        functools.partial(_expert_kernel, bm=bm),
        grid_spec=grid_spec,
        out_shape=jax.ShapeDtypeStruct(xs.shape, jnp.uint32),
        compiler_params=_cparams("arbitrary"),
        name="moe_experts",
    )(blk_e, used, first, nxt, xs, w_gu, w_down)


def _combine_kernel(dest_ref, gate_ref, base_ref, gf_ref, fn_ref, ys_ref, o_ref, buf, sem):
    n = base_ref.shape[0]
    chunks = buf.shape[1] // n

    def body(g, carry):
        r0 = pl.multiple_of(g * SUBLANES, SUBLANES)
        for j in range(SUBLANES):
            for k in range(TOP_K):
                pltpu.make_async_copy(_token_rows(ys_ref, dest_ref[k, r0 + j], chunks),
                                      _token_rows(buf.at[k], r0 + j, chunks), sem).start()
        return carry

    lax.fori_loop(0, n // SUBLANES, body, 0)
    for k in range(TOP_K):
        _row_copy_wait(ys_ref, buf.at[k], sem, n * chunks)
    gates = gate_ref[...]
    lo_acc = None
    for k in range(TOP_K):
        lo, hi = _unpack_bf16_pairs(_load_chunked(buf.at[k], n))
        g = gates[:, k:k + 1]
        lo_acc = g * lo if lo_acc is None else lo_acc + g * lo
        hi_acc = g * hi if k == 0 else hi_acc + g * hi
    routed = jnp.concatenate([lo_acc, hi_acc], axis=1)
    o_ref[...] = _rms(base_ref[...] + gf_ref[...] * routed, fn_ref[...])


def _combine(dest, gates, base, gf, fn, ys):
    nb, n, d = base.shape
    rows = dest.shape[-1]
    mrows = gf.shape[1]
    mod = pl.BlockSpec((None, 1 if mrows == 1 else rows, d),
                       (lambda b, i: (b, 0, 0)) if mrows == 1 else (lambda b, i: (b, i, 0)))
    return pl.pallas_call(
        _combine_kernel,
        grid=(nb, n // rows),
        in_specs=[pl.BlockSpec((None, None, TOP_K, rows), lambda b, i: (b, i, 0, 0), memory_space=pltpu.SMEM),
                  pl.BlockSpec((None, rows, TOP_K), lambda b, i: (b, i, 0)),
                  pl.BlockSpec((None, rows, d), lambda b, i: (b, i, 0)),
                  mod,
                  pl.BlockSpec(fn.shape, lambda b, i: (0, 0)),
                  pl.BlockSpec(memory_space=pl.ANY)],
        out_specs=pl.BlockSpec((None, rows, d), lambda b, i: (b, i, 0)),
        out_shape=jax.ShapeDtypeStruct((nb, n, d), F32),
        scratch_shapes=[pltpu.VMEM((TOP_K, rows * (d // 2 // LANES), LANES), jnp.uint32),
                        pltpu.SemaphoreType.DMA(())],
        compiler_params=_cparams("arbitrary", "arbitrary"),
        name="moe_combine",
    )(dest, gates, base, gf, fn, ys)


def _dispatch_plan(tile_counts, n_assign, bm):
    n_exp = tile_counts.shape[1]
    counts = jnp.sum(tile_counts, axis=0)
    padded = (counts + bm - 1) // bm * bm
    pad_end = jnp.cumsum(padded)
    pad_start = pad_end - padded
    base = pad_start[None, :] + jnp.cumsum(tile_counts, axis=0) - tile_counts
    n_blocks = -(-n_assign // bm) + n_exp
    first_row = jnp.arange(n_blocks, dtype=jnp.int32) * bm
    blk_e = jnp.minimum(jnp.sum((pad_end[None, :] <= first_row[:, None]).astype(jnp.int32), axis=1), n_exp - 1)
    used = pad_end[-1] // bm
    blk = jnp.arange(n_blocks, dtype=jnp.int32)
    valid = blk < used
    is_first = valid & ((blk == 0) | (blk_e != jnp.roll(blk_e, 1)))
    group = jnp.cumsum(is_first.astype(jnp.int32)) - 1
    first = jnp.where(is_first, 1 + group % 2, 0)
    later = valid[None, :] & (blk_e[None, :] > blk_e[:, None])
    nxt = jnp.min(jnp.where(later, blk_e[None, :], n_exp), axis=1)
    nxt = jnp.where(nxt < n_exp, nxt, -1)
    i32 = lambda a: a.astype(jnp.int32)
    return base.astype(F32), i32(blk_e), i32(used).reshape(1), i32(first), i32(nxt), n_blocks


def _rel_bucket(rel):
    nb = NUM_BUCKETS // 2
    max_exact = nb // 2
    n = jnp.abs(rel)
    nf = jnp.maximum(n, 1).astype(F32)
    large = max_exact + (jnp.log(nf / max_exact) / math.log(MAX_DISTANCE / max_exact)
                         * (nb - max_exact)).astype(jnp.int32)
    large = jnp.minimum(large, nb - 1)
    return jnp.where(rel > 0, nb, 0) + jnp.where(n < max_exact, n, large)


def _bias_table(rel_bias, q_pos, k_pos, masked, key_major=False):
    rel = jnp.asarray(k_pos, jnp.int32)[None, :] - jnp.asarray(q_pos, jnp.int32)[:, None]
    bucket = _rel_bucket(rel)[None, None]
    tab = rel_bias.astype(F32) * LOG2E
    bias = jnp.zeros(tab.shape[1:] + rel.shape, F32)
    for i in range(tab.shape[0]):
        bias = jnp.where(bucket == i, tab[i][:, :, None, None], bias)
    if masked:
        vis = (np.asarray(k_pos)[None, :] // CHUNK) <= (np.asarray(q_pos)[:, None] // CHUNK)
        bias = jnp.where(jnp.asarray(vis), bias, NEG_INF)
    return jnp.swapaxes(bias, -1, -2) if key_major else bias


def _rope_tables(pos):
    half = MLA_ROPE // 2
    inv = ROPE_THETA ** (-jnp.arange(half, dtype=F32) / half)
    ang = pos.astype(F32)[:, None] * inv
    cos, sin = jnp.cos(ang), jnp.sin(ang)
    n = pos.shape[0]
    z = lambda w: jnp.zeros((n, w), F32)
    cc = jnp.concatenate([cos, cos], axis=1)
    ss = jnp.concatenate([-sin, sin], axis=1)
    ck = jnp.concatenate([cc, z(HEAD_PAD - MLA_ROPE)], axis=1)
    sk = jnp.concatenate([ss, z(HEAD_PAD - MLA_ROPE)], axis=1)
    qs = MLA_SCALE * LOG2E
    cq = qs * jnp.concatenate([cc, jnp.ones((n, MLA_NOPE), F32), z(HEAD_PAD - MLA_ROPE - MLA_NOPE)], axis=1)
    sq = qs * sk
    return ck, sk, cq, sq


def _layer_weights(w_in, w_uq, w_ukv, q_lora, kv_lora, diff_qk, diff_w):
    d = w_in.shape[0]
    half = MLA_ROPE // 2
    swap = np.concatenate([np.arange(half, MLA_ROPE), np.arange(half)])
    o = q_lora + kv_lora
    w_kpe = w_in[:, o:o + MLA_ROPE]
    rest = w_in[:, o + MLA_ROPE:]
    zpad = jnp.zeros((d, HEAD_PAD - MLA_ROPE), w_in.dtype)
    win = jnp.concatenate([w_in[:, :o], rest, w_kpe, zpad, w_kpe[:, swap], zpad], axis=1).astype(BF16)
    r = w_uq.shape[0]
    wq3 = w_uq.reshape(r, MLA_HEADS, MLA_NOPE + MLA_ROPE)
    nope, rope = wq3[..., :MLA_NOPE], wq3[..., MLA_NOPE:]
    z3 = lambda w: jnp.zeros((r, MLA_HEADS, w), w_uq.dtype)
    wq_a = jnp.concatenate([rope, nope, z3(HEAD_PAD - MLA_ROPE - MLA_NOPE)], axis=-1)
    wq_s = jnp.concatenate([rope[..., swap], z3(HEAD_PAD - MLA_ROPE)], axis=-1)
    wq = jnp.concatenate([wq_a.reshape(r, -1), wq_s.reshape(r, -1)], axis=1).astype(BF16)
    rk = w_ukv.shape[0]
    wkv3 = w_ukv.reshape(rk, MLA_HEADS, MLA_NOPE + MLA_V)
    zk = lambda w: jnp.zeros((rk, MLA_HEADS, w), w_ukv.dtype)
    wk = jnp.concatenate([zk(MLA_ROPE), wkv3[..., :MLA_NOPE], zk(HEAD_PAD - MLA_ROPE - MLA_NOPE)], axis=-1)
    wk = wk.reshape(rk, -1).astype(BF16)
    wv = wkv3[..., MLA_NOPE:].reshape(rk, -1).astype(BF16)
    return win, wq, wk, wv


def kernel(x_prompt, x_sample, cache_mla_latent, cache_mla_krope, cache_diff_k, cache_diff_v, c_prompt, c_sample, w_ada, b_ada, norm_attn, w_in, mla_q_norm, mla_kv_norm, w_uq, w_ukv, lambda_q1, lambda_k1, lambda_q2, lambda_k2, diff_subln, rel_bias, w_out, norm_ffn, w_router, router_bias, w_exp_gu, w_exp_down, w_shared_gu, w_shared_down, final_norm):
    b, s, d = x_prompt.shape
    bd, sd, _ = x_sample.shape
    depth = w_ada.shape[0]
    past = cache_mla_latent.shape[2]
    q_lora = mla_q_norm.shape[1]
    kv_lora = mla_kv_norm.shape[1]
    diff_qk = DIFF_HEADS * 2 * DIFF_DH
    diff_w = DIFF_HEADS * DIFF_V
    dims = (q_lora, kv_lora, diff_qk, diff_w)
    n_exp = w_router.shape[2]
    ns = bd * sd
    tile = min(ATTN_TILE, s)
    assert s % tile == 0 and tile % CHUNK == 0 and tile >= MAX_DISTANCE

    pos_p = np.arange(s)
    pos_s = past + np.arange(sd)
    tabs_p = _rope_tables(jnp.asarray(pos_p, jnp.int32))
    tabs_s = tuple(jnp.tile(t, (bd, 1)) for t in _rope_tables(jnp.asarray(pos_s, jnp.int32)))
    bias_p = jnp.stack([_bias_table(rel_bias, pos_p[:tile], pos_p[:tile], True, key_major=True),
                        _bias_table(rel_bias, pos_p[:tile] + tile, pos_p[:tile], False, key_major=True)], axis=2)
    far_p = _bias_table(rel_bias, np.array([2 * tile]), np.array([0]), False)[:, :, 0, 0]
    bias_sc = _bias_table(rel_bias, pos_s, np.arange(past), True)
    bias_sn = _bias_table(rel_bias, pos_s, pos_s, True)
    mask_sn = jnp.asarray(np.where((pos_s[None, :] // CHUNK) <= (pos_s[:, None] // CHUNK), 0.0, NEG_INF), F32)
    assert bool(np.all((np.arange(past)[None, :] // CHUNK) <= (pos_s[:, None] // CHUNK)))

    xp, xs_tok = x_prompt, x_sample.reshape(1, ns, d)
    c_all = jnp.concatenate([c_prompt, c_sample], axis=0)
    row2 = lambda a: a.reshape(1, -1)
    new_p, new_s = [], []
    for l in range(depth):
        lam_init = 0.8 - 0.6 * math.exp(-0.3 * l)
        mod = _ada(c_all, w_ada[l], b_ada[l])
        mods = jnp.split(mod, 6, axis=-1)
        mod_p = [m[:b].reshape(b, 1, d) for m in mods]
        mod_s = [jnp.repeat(m[b:], sd, axis=0).reshape(1, ns, d) for m in mods]
        win, wq, wk, wv = _layer_weights(w_in[l], w_uq[l], w_ukv[l], *dims)
        wkv = jnp.concatenate([wk, wv], axis=1)
        lam_vecs = (row2(lambda_q1[l]), row2(lambda_k1[l]), row2(lambda_q2[l]), row2(lambda_k2[l]))
        subln = row2(diff_subln[l])
        shared_pre = (row2(norm_attn[l]), win, row2(mla_q_norm[l]), row2(mla_kv_norm[l]), wq, wkv)

        lat, kpe, dk, dv, q, k, v_t, dqb, dkb, dv_t = _pre(xp, mod_p[1], mod_p[0], *shared_pre, *tabs_p, dims, tile)
        new_p.append((lat, kpe, dk, dv))
        om_p = _mla_prompt(q, k, v_t)
        od_p = _diff_prompt(far_p, dqb, dkb, dv_t, bias_p, *lam_vecs, subln, lam_init)

        lat_s, kpe_s, dk_s, dv_s, q_s, _, _, dqb_s, _, _ = _pre(
            xs_tok, mod_s[1], mod_s[0], *shared_pre, *tabs_s, dims, min(ROW_TILE, ns))
        new_s.append((lat_s, kpe_s, dk_s, dv_s))
        per = lambda a: a.reshape(bd, sd, a.shape[-1])
        om_s = _mla_sample(per(q_s), cache_mla_latent[l], cache_mla_krope[l], per(lat_s), per(kpe_s), wk, wv,
                           mask_sn)
        od_s = _diff_sample(per(dqb_s), cache_diff_k[l].reshape(bd, past, diff_qk),
                            cache_diff_v[l].reshape(bd, past, diff_w), per(dk_s), per(dv_s),
                            bias_sc, bias_sn, *lam_vecs, subln, lam_init)
        om_s = om_s.reshape(1, ns, -1)
        od_s = od_s.reshape(1, ns, -1)

        wr_t = w_router[l].T
        wrh = wr_t.astype(BF16)
        wrl = (wr_t - wrh.astype(F32)).astype(BF16)
        post_w = (row2(norm_ffn[l]), w_out[l].astype(BF16), wrh, wrl, router_bias[l].reshape(n_exp, 1),
                  w_shared_gu[l].astype(BF16), w_shared_down[l].astype(BF16))
        base_p, h2p_p, idx_p, gate_p, rank_p, cnt_p = _post(
            xp, om_p, od_p, mod_p[2], mod_p[4], mod_p[3], mod_p[5], *post_w)
        base_s, h2p_s, idx_s, gate_s, rank_s, cnt_s = _post(
            xs_tok, om_s, od_s, mod_s[2], mod_s[4], mod_s[3], mod_s[5], *post_w)

        tp = b * s
        tiles_p = cnt_p.shape[0] * cnt_p.shape[1]
        tile_counts = jnp.concatenate([cnt_p.reshape(-1, n_exp), cnt_s.reshape(-1, n_exp)], axis=0)
        slot0, blk_e, used, first, nxt, n_blocks = _dispatch_plan(tile_counts, (tp + ns) * TOP_K, EXPERT_BLOCK)
        dest_p = _dest(idx_p, rank_p, slot0[:tiles_p].reshape(b, -1, n_exp, 1))
        dest_s = _dest(idx_s, rank_s, slot0[tiles_p:].reshape(1, -1, n_exp, 1))
        slots = jnp.zeros((n_blocks * EXPERT_BLOCK * (d // 2 // LANES), LANES), jnp.uint32)
        slots = _dispatch(dest_p.reshape((-1,) + dest_p.shape[2:]), h2p_p.reshape(-1, LANES), slots)
        slots = _dispatch(dest_s.reshape((-1,) + dest_s.shape[2:]), h2p_s.reshape(-1, LANES), slots)
        ys = _experts(blk_e, used, first, nxt, slots, w_exp_gu[l], w_exp_down[l])
        fn = row2(final_norm) if l == depth - 1 else jnp.ones((1, d), F32)
        assert depth == 1
        xp = _combine(dest_p, jnp.transpose(gate_p, (0, 2, 1)), base_p, mod_p[5], fn, ys)
        xs_tok = _combine(dest_s, jnp.transpose(gate_s, (0, 2, 1)), base_s, mod_s[5], fn, ys)

    stack = lambda rows, i, shape: (rows[0][i] if depth == 1 else jnp.stack([r[i] for r in rows])).reshape(shape)
    outs_p = (stack(new_p, 0, (depth, b, s, kv_lora)), stack(new_p, 1, (depth, b, s, MLA_ROPE)),
              stack(new_p, 2, (depth, b, s, DIFF_HEADS, 2, DIFF_DH)), stack(new_p, 3, (depth, b, s, DIFF_HEADS, DIFF_V)))
    outs_s = (stack(new_s, 0, (depth, bd, sd, kv_lora)), stack(new_s, 1, (depth, bd, sd, MLA_ROPE)),
              stack(new_s, 2, (depth, bd, sd, DIFF_HEADS, 2, DIFF_DH)),
              stack(new_s, 3, (depth, bd, sd, DIFF_HEADS, DIFF_V)))
    return (xp, xs_tok.reshape(bd, sd, d)) + outs_p + outs_s
```
